```python
import jax, jax.numpy as jnp
from jax import lax
import numpy as np

D_MODEL = 2048
BATCH = 4
SEQ = 4096
DEPTH = 4

N_A = DEPTH // 2
N_B = DEPTH - N_A
POOL_WINDOWS = (2, 4, 8, 16)
N_POOL_GROUPS = len(POOL_WINDOWS)
POOL_GROUP_DIM = D_MODEL // N_POOL_GROUPS
QK_NOPE_DIM = 128
QK_ROPE_DIM = 64
V_HEAD_DIM = 128
N_HEADS = D_MODEL // V_HEAD_DIM
KV_LORA_RANK = D_MODEL // 4
Q_LORA_RANK = ((1536 * D_MODEL // 7168 + 127) // 128) * 128
Q_HEAD_DIM = QK_NOPE_DIM + QK_ROPE_DIM
SM_SCALE = Q_HEAD_DIM ** -0.5
ROPE_THETA = 10000.0
Q_BLOCK = 128
FFN_DIM = ((8 * D_MODEL // 3 + 255) // 256) * 256
N_MOD = 6
EPS = 1e-6

kernel_name = "yoco_pool_mla_adaln_trunk"


def rmsnorm(x, g):
    x32 = x.astype(jnp.float32)
    y = x32 * lax.rsqrt(jnp.mean(x32 * x32, axis=-1, keepdims=True) + EPS)
    return (y * g.astype(jnp.float32)).astype(x.dtype)


def modulate(h, shift, scale):
    return h * (1.0 + scale[:, None, :]) + shift[:, None, :]


def rope_tables(positions):
    inv_freq = 1.0 / (ROPE_THETA ** (jnp.arange(0, QK_ROPE_DIM, 2, dtype=jnp.float32) / QK_ROPE_DIM))
    ang = positions.astype(jnp.float32)[..., None] * inv_freq
    return jnp.cos(ang), jnp.sin(ang)


def apply_rope(t, cos, sin):
    half = t.shape[-1] // 2
    t1, t2 = t[..., :half], t[..., half:]
    return jnp.concatenate([t1 * cos - t2 * sin, t2 * cos + t1 * sin], axis=-1).astype(t.dtype)


def pool_mixer(h, w_grp, scale):
    B, S, D = h.shape
    hf = h.astype(jnp.float32)
    csum = jnp.cumsum(hf, axis=1)
    t = jnp.arange(S)
    outs = []
    for g, w in enumerate(POOL_WINDOWS):
        lo, hi = g * POOL_GROUP_DIM, (g + 1) * POOL_GROUP_DIM
        cs = csum[..., lo:hi]
        lag = jnp.pad(cs, ((0, 0), (w, 0), (0, 0)))[:, :S]
        cnt = jnp.minimum(t + 1, w).astype(jnp.float32)[None, :, None]
        outs.append((cs - lag) / cnt - hf[..., lo:hi])
    d = jnp.stack(outs, axis=2).astype(h.dtype)
    y = jnp.einsum('bsgc,gcd->bsgd', d, w_grp).reshape(B, S, D)
    return y * scale


def swiglu(h, w_gate, w_up, w_down):
    return (jax.nn.silu(h @ w_gate) * (h @ w_up)) @ w_down


def shared_kv(h, w_dkv, kv_norm, w_uk, w_uv, w_kr, cos, sin):
    B, S, _ = h.shape
    ckv = rmsnorm(h @ w_dkv, kv_norm)
    k_nope = (ckv @ w_uk).reshape(B, S, N_HEADS, QK_NOPE_DIM)
    v = (ckv @ w_uv).reshape(B, S, N_HEADS, V_HEAD_DIM)
    k_rope = apply_rope(h @ w_kr, cos, sin)
    return k_nope, k_rope, v


def causal_mla_attention(q_nope, q_rope, k_nope, k_rope, v):
    B, S, H, _ = q_nope.shape
    nblk = S // Q_BLOCK
    key_idx = jnp.arange(S)

    def to_blocks(t):
        return jnp.moveaxis(t.reshape((B, nblk, Q_BLOCK) + t.shape[2:]), 1, 0)

    def one_block(args):
        qn, qr, blk = args
        s = (jnp.einsum('bqhd,bkhd->bhqk', qn, k_nope, preferred_element_type=jnp.float32)
             + jnp.einsum('bqhr,bkr->bhqk', qr, k_rope, preferred_element_type=jnp.float32))
        q_idx = blk * Q_BLOCK + jnp.arange(Q_BLOCK)
        mask = key_idx[None, :] <= q_idx[:, None]
        p = jax.nn.softmax(jnp.where(mask, s * SM_SCALE, -jnp.inf), axis=-1)
        return jnp.einsum('bhqk,bkhd->bqhd', p.astype(v.dtype), v)

    o = lax.map(one_block, (to_blocks(q_nope), to_blocks(q_rope), jnp.arange(nblk)))
    return jnp.moveaxis(o, 0, 1).reshape(B, S, H * V_HEAD_DIM)


def mla_mixer(h, kv, w_dq, q_norm, w_uq, w_o, cos, sin):
    B, S, _ = h.shape
    cq = rmsnorm(h @ w_dq, q_norm)
    q = (cq @ w_uq).reshape(B, S, N_HEADS, Q_HEAD_DIM)
    q_nope = q[..., :QK_NOPE_DIM]
    q_rope = apply_rope(q[..., QK_NOPE_DIM:], cos[:, :, None, :], sin[:, :, None, :])
    k_nope, k_rope, v = kv
    o = causal_mla_attention(q_nope, q_rope, k_nope, k_rope, v)
    return o @ w_o


def setup_inputs(seed: int = 0) -> dict:
    key = jax.random.key(seed)
    ks = jax.random.split(key, 32)
    f32 = jnp.float32

    def nrm(k, shape, std):
        return jax.random.normal(k, shape, f32) * std

    def gain(k, shape):
        return 1.0 + 0.02 * jax.random.normal(k, shape, f32)

    D, F, H = D_MODEL, FFN_DIM, N_HEADS
    positions = (jax.random.randint(ks[2], (BATCH, 1), 0, 1024, dtype=jnp.int32)
                 + jnp.arange(SEQ, dtype=jnp.int32)[None, :])
    return {
        "x": nrm(ks[0], (BATCH, SEQ, D), 1.0),
        "c": nrm(ks[1], (BATCH, D), 1.0),
        "positions": positions,
        "mod_w": nrm(ks[3], (DEPTH, D, N_MOD * D), 0.5 * D ** -0.5),
        "mod_b": nrm(ks[4], (DEPTH, N_MOD * D), 0.02),
        "norm_mix": gain(ks[5], (DEPTH, D)),
        "norm_ffn": gain(ks[6], (DEPTH, D)),
        "pool_w": nrm(ks[7], (N_A, N_POOL_GROUPS, POOL_GROUP_DIM, POOL_GROUP_DIM), POOL_GROUP_DIM ** -0.5),
        "pool_scale": gain(ks[8], (N_A, D)),
        "kv_mod_w": nrm(ks[9], (D, 2 * D), 0.5 * D ** -0.5),
        "kv_mod_b": nrm(ks[10], (2 * D,), 0.02),
        "kv_in_norm": gain(ks[11], (D,)),
        "w_dkv": nrm(ks[12], (D, KV_LORA_RANK), D ** -0.5),
        "kv_norm": gain(ks[13], (KV_LORA_RANK,)),
        "w_uk": nrm(ks[14], (KV_LORA_RANK, H * QK_NOPE_DIM), KV_LORA_RANK ** -0.5),
        "w_uv": nrm(ks[15], (KV_LORA_RANK, H * V_HEAD_DIM), KV_LORA_RANK ** -0.5),
        "w_kr": nrm(ks[16], (D, QK_ROPE_DIM), D ** -0.5),
        "w_dq": nrm(ks[17], (N_B, D, Q_LORA_RANK), D ** -0.5),
        "q_norm": gain(ks[18], (N_B, Q_LORA_RANK)),
        "w_uq": nrm(ks[19], (N_B, Q_LORA_RANK, H * Q_HEAD_DIM), Q_LORA_RANK ** -0.5),
        "w_o": nrm(ks[20], (N_B, H * V_HEAD_DIM, D), (H * V_HEAD_DIM) ** -0.5),
        "ffn_gate": nrm(ks[21], (DEPTH, D, F), D ** -0.5),
        "ffn_up": nrm(ks[22], (DEPTH, D, F), D ** -0.5),
        "ffn_down": nrm(ks[23], (DEPTH, F, D), F ** -0.5),
        "final_norm": gain(ks[24], (D,)),
    }


def reference(x, c, positions, mod_w, mod_b, norm_mix, norm_ffn, pool_w, pool_scale,
              kv_mod_w, kv_mod_b, kv_in_norm, w_dkv, kv_norm, w_uk, w_uv, w_kr,
              w_dq, q_norm, w_uq, w_o, ffn_gate, ffn_up, ffn_down, final_norm):
    cos, sin = rope_tables(positions)
    sc = jax.nn.silu(c)
    kv = None
    for i in range(DEPTH):
        shift_m, scale_m, gate_m, shift_f, scale_f, gate_f = jnp.split(sc @ mod_w[i] + mod_b[i], N_MOD, axis=-1)
        if i == N_A:
            kv_shift, kv_scale = jnp.split(sc @ kv_mod_w + kv_mod_b, 2, axis=-1)
            h_kv = modulate(rmsnorm(x, kv_in_norm), kv_shift, kv_scale)
            kv = shared_kv(h_kv, w_dkv, kv_norm, w_uk, w_uv, w_kr, cos, sin)
        h = modulate(rmsnorm(x, norm_mix[i]), shift_m, scale_m)
        if i < N_A:
            y = pool_mixer(h, pool_w[i], pool_scale[i])
        else:
            j = i - N_A
            y = mla_mixer(h, kv, w_dq[j], q_norm[j], w_uq[j], w_o[j], cos, sin)
        x = x + gate_m[:, None, :] * y
        h = modulate(rmsnorm(x, norm_ffn[i]), shift_f, scale_f)
        x = x + gate_f[:, None, :] * swiglu(h, ffn_gate[i], ffn_up[i], ffn_down[i])
    return rmsnorm(x, final_norm)
```

```python
import functools
import math

import numpy as np
import jax
import jax.numpy as jnp
from jax import lax
from jax.experimental import pallas as pl
from jax.experimental.pallas import tpu as pltpu

D_MODEL = 2048
DEPTH = 4
N_A = DEPTH // 2
POOL_WINDOWS = (2, 4, 8, 16)
N_POOL_GROUPS = len(POOL_WINDOWS)
POOL_GROUP_DIM = D_MODEL // N_POOL_GROUPS
POOL_HALO = 16
QK_NOPE_DIM = 128
QK_ROPE_DIM = 64
V_HEAD_DIM = 128
N_HEADS = D_MODEL // V_HEAD_DIM
KV_LORA_RANK = D_MODEL // 4
Q_LORA_RANK = 512
Q_HEAD_DIM = QK_NOPE_DIM + QK_ROPE_DIM
HEAD_SLOT = 256
SM_SCALE = Q_HEAD_DIM ** -0.5
ROPE_THETA = 10000.0
FFN_DIM = 5632
N_MOD = 6
EPS = 1e-6
LOG2E = math.log2(math.e)
NEG_BIG = -1e30

LANES = 128
VMEM_LIMIT = 56 * 1024 * 1024

BF16 = jnp.bfloat16
F32 = jnp.float32


def _params(*semantics):
    return pltpu.CompilerParams(dimension_semantics=semantics, vmem_limit_bytes=VMEM_LIMIT)


def _dot(a, b):
    return jnp.dot(a, b, preferred_element_type=F32)


def _norm_mod(x, gain, shift):
    ms = jnp.mean(x * x, axis=-1, keepdims=True)
    return x * lax.rsqrt(ms + EPS) * gain + shift


def _rmsnorm(x, g):
    ms = jnp.mean(x * x, axis=-1, keepdims=True)
    return x * lax.rsqrt(ms + EPS) * g


def _mod_kernel(c_ref, w_ref, b_ref, o_ref):
    c = c_ref[...]
    sc = (c * jax.nn.sigmoid(c)).astype(BF16)
    o_ref[...] = _dot(sc, w_ref[...].astype(BF16)) + b_ref[...]


def _modulation(c_rows, w, b, tn):
    n_layers, d, n = w.shape
    rows = c_rows.shape[0]
    return pl.pallas_call(
        _mod_kernel,
        grid=(n_layers, n // tn),
        in_specs=[
            pl.BlockSpec((rows, d), lambda l, j: (0, 0)),
            pl.BlockSpec((None, d, tn), lambda l, j: (l, 0, j)),
            pl.BlockSpec((None, 1, tn), lambda l, j: (l, 0, j)),
        ],
        out_specs=pl.BlockSpec((None, rows, tn), lambda l, j: (l, 0, j)),
        out_shape=jax.ShapeDtypeStruct((n_layers, rows, n), F32),
        compiler_params=_params("parallel", "parallel"),
        name="modulation",
    )(c_rows, w, b)


def _rope_table_kernel(pos_ref, freq_ref, o_ref):
    ang = pos_ref[0].astype(F32) * freq_ref[...]
    lane = lax.broadcasted_iota(jnp.int32, ang.shape, 1)
    sin = jnp.sin(ang)
    o_ref[0] = jnp.where(lane < 64, jnp.cos(ang), jnp.where(lane < 96, -sin, sin))


def _rope_table(positions, ts):
    b, s = positions.shape
    half = QK_ROPE_DIM // 2
    inv_freq = 1.0 / (ROPE_THETA ** (np.arange(0, QK_ROPE_DIM, 2, dtype=np.float32) / QK_ROPE_DIM))
    freq = jnp.asarray(np.tile(inv_freq.astype(np.float32), LANES // half)[None, :])
    return pl.pallas_call(
        _rope_table_kernel,
        grid=(b, s // ts),
        in_specs=[
            pl.BlockSpec((1, ts, 1), lambda i, t: (i, t, 0)),
            pl.BlockSpec((1, LANES), lambda i, t: (0, 0)),
        ],
        out_specs=pl.BlockSpec((1, ts, LANES), lambda i, t: (i, t, 0)),
        out_shape=jax.ShapeDtypeStruct((b, s, LANES), F32),
        compiler_params=_params("parallel", "parallel"),
        name="rope_table",
    )(positions.reshape(b, s, 1), freq)


def _apply_rope(t2, table):
    prod = t2 * table
    return prod + pltpu.roll(prod, 64, axis=1)


def _pool_kernel(x_ref, halo_ref, shift_ref, scale_ref, gate_ref, g_ref, w_ref, ps_ref, o_ref):
    t = pl.program_id(1)
    ts = x_ref.shape[1]
    x = x_ref[0]
    gain = g_ref[...] * (1.0 + scale_ref[0])
    shift = shift_ref[0]
    h = _norm_mod(x, gain, shift)
    hh = _norm_mod(halo_ref[0], gain, shift) * (t > 0).astype(F32)
    hs = jnp.concatenate([hh, h], axis=0)
    row = t * ts + lax.broadcasted_iota(jnp.int32, (ts, POOL_GROUP_DIM), 0)
    out_scale = gate_ref[0] * ps_ref[...]
    for g, w in enumerate(POOL_WINDOWS):
        lo, hi = g * POOL_GROUP_DIM, (g + 1) * POOL_GROUP_DIM
        s = hs[:, lo:hi]
        k = 1
        while k < w:
            s = s + pltpu.roll(s, k, axis=0)
            k *= 2
        cnt = jnp.minimum(row + 1, w).astype(F32)
        d = s[POOL_HALO:, :] / cnt - h[:, lo:hi]
        y = _dot(d.astype(BF16), w_ref[g])
        o_ref[0, :, lo:hi] = x[:, lo:hi] + out_scale[:, lo:hi] * y


def _pool_layer(x, shift, scale, gate, norm_w, pool_w, pool_scale, ts):
    b, s, d = x.shape
    halo_blocks = ts // POOL_HALO
    vec = pl.BlockSpec((1, 1, d), lambda i, t: (i, 0, 0))
    row = pl.BlockSpec((1, d), lambda i, t: (0, 0))
    return pl.pallas_call(
        _pool_kernel,
        grid=(b, s // ts),
        in_specs=[
            pl.BlockSpec((1, ts, d), lambda i, t: (i, t, 0)),
            pl.BlockSpec((1, POOL_HALO, d), lambda i, t: (i, jnp.maximum(t * halo_blocks - 1, 0), 0)),
            vec, vec, vec, row,
            pl.BlockSpec((N_POOL_GROUPS, POOL_GROUP_DIM, POOL_GROUP_DIM), lambda i, t: (0, 0, 0)),
            row,
        ],
        out_specs=pl.BlockSpec((1, ts, d), lambda i, t: (i, t, 0)),
        out_shape=jax.ShapeDtypeStruct((b, s, d), F32),
        compiler_params=_params("parallel", "parallel"),
        name="pool_layer",
    )(x, x, shift, scale, gate, norm_w, pool_w, pool_scale)


def _ffn_kernel(x_ref, shift_ref, scale_ref, gate_ref, g_ref, wg_ref, wu_ref, wd_ref, fn_ref,
                o_ref, h_scr, acc_scr, *, final_norm):
    f = pl.program_id(2)

    @pl.when(f == 0)
    def _():
        gain = g_ref[...] * (1.0 + scale_ref[0])
        h_scr[...] = _norm_mod(x_ref[0], gain, shift_ref[0]).astype(BF16)

    h = h_scr[...]
    a = _dot(h, wg_ref[...])
    u = _dot(h, wu_ref[...])
    act = (a * jax.nn.sigmoid(a) * u).astype(BF16)
    part = _dot(act, wd_ref[...])

    @pl.when(f == 0)
    def _():
        acc_scr[...] = part

    @pl.when(f > 0)
    def _():
        acc_scr[...] += part

    @pl.when(f == pl.num_programs(2) - 1)
    def _():
        out = x_ref[0] + gate_ref[0] * acc_scr[...]
        if final_norm:
            out = _rmsnorm(out, fn_ref[...])
        o_ref[0] = out


def _ffn_layer(x, shift, scale, gate, norm_w, wg, wu, wd, final_w, tm, final_norm):
    b, s, d = x.shape
    nf, _, tf = wg.shape
    vec = pl.BlockSpec((1, 1, d), lambda i, t, f: (i, 0, 0))
    row = pl.BlockSpec((1, d), lambda i, t, f: (0, 0))
    return pl.pallas_call(
        functools.partial(_ffn_kernel, final_norm=final_norm),
        grid=(b, s // tm, nf),
        in_specs=[
            pl.BlockSpec((1, tm, d), lambda i, t, f: (i, t, 0)),
            vec, vec, vec, row,
            pl.BlockSpec((None, d, tf), lambda i, t, f: (f, 0, 0)),
            pl.BlockSpec((None, d, tf), lambda i, t, f: (f, 0, 0)),
            pl.BlockSpec((tf, d), lambda i, t, f: (f, 0)),
            row,
        ],
        out_specs=pl.BlockSpec((1, tm, d), lambda i, t, f: (i, t, 0)),
        out_shape=jax.ShapeDtypeStruct((b, s, d), F32),
        scratch_shapes=[pltpu.VMEM((tm, d), BF16), pltpu.VMEM((tm, d), F32)],
        compiler_params=_params("parallel", "parallel", "arbitrary"),
        name="ffn_layer",
    )(x, shift, scale, gate, norm_w, wg, wu, wd, final_w)


def _kv_kernel(x_ref, shift_ref, scale_ref, g_ref, wdkv_ref, kvn_ref, wuk_ref, wuv_ref, wkr_ref,
               tab_ref, k_ref, v_ref):
    gain = g_ref[...] * (1.0 + scale_ref[0])
    h = _norm_mod(x_ref[0], gain, shift_ref[0]).astype(BF16)
    ckv = _rmsnorm(_dot(h, wdkv_ref[...]), kvn_ref[...]).astype(BF16)
    kn = _dot(ckv, wuk_ref[...]).astype(BF16)
    v = _dot(ckv, wuv_ref[...]).astype(BF16)
    kr = _apply_rope(_dot(h, wkr_ref[...]), tab_ref[0])
    lane = lax.broadcasted_iota(jnp.int32, kr.shape, 1)
    kr = jnp.where(lane < QK_ROPE_DIM, kr, 0.0).astype(BF16)
    for hd in range(N_HEADS):
        k_ref[0, hd, :, :QK_NOPE_DIM] = kn[:, hd * QK_NOPE_DIM:(hd + 1) * QK_NOPE_DIM]
        k_ref[0, hd, :, QK_NOPE_DIM:] = kr
        v_ref[0, hd] = v[:, hd * V_HEAD_DIM:(hd + 1) * V_HEAD_DIM]


def _shared_kv(x, shift, scale, norm_w, w_dkv, kv_norm, w_uk, w_uv, w_kr2, table, ts):
    b, s, d = x.shape
    full = lambda a: pl.BlockSpec(a.shape, lambda i, t: (0,) * a.ndim)
    vec = pl.BlockSpec((1, 1, d), lambda i, t: (i, 0, 0))
    return pl.pallas_call(
        _kv_kernel,
        grid=(b, s // ts),
        in_specs=[
            pl.BlockSpec((1, ts, d), lambda i, t: (i, t, 0)),
            vec, vec, full(norm_w), full(w_dkv), full(kv_norm), full(w_uk), full(w_uv), full(w_kr2),
            pl.BlockSpec((1, ts, LANES), lambda i, t: (i, t, 0)),
        ],
        out_specs=[
            pl.BlockSpec((1, N_HEADS, ts, HEAD_SLOT), lambda i, t: (i, 0, t, 0)),
            pl.BlockSpec((1, N_HEADS, ts, V_HEAD_DIM), lambda i, t: (i, 0, t, 0)),
        ],
        out_shape=[
            jax.ShapeDtypeStruct((b, N_HEADS, s, HEAD_SLOT), BF16),
            jax.ShapeDtypeStruct((b, N_HEADS, s, V_HEAD_DIM), BF16),
        ],
        compiler_params=_params("parallel", "parallel"),
        name="shared_kv",
    )(x, shift, scale, norm_w, w_dkv, kv_norm, w_uk, w_uv, w_kr2, table)


def _q_kernel(x_ref, shift_ref, scale_ref, g_ref, wdq_ref, qn_ref, wuq_ref, tab_ref, q_ref):
    gain = g_ref[...] * (1.0 + scale_ref[0])
    h = _norm_mod(x_ref[0], gain, shift_ref[0]).astype(BF16)
    cq = _rmsnorm(_dot(h, wdq_ref[...]), qn_ref[...]).astype(BF16)
    table = tab_ref[0] * (SM_SCALE * LOG2E)
    for hd in range(N_HEADS):
        q = _dot(cq, wuq_ref[:, hd * HEAD_SLOT:(hd + 1) * HEAD_SLOT])
        q_ref[0, hd, :, :QK_NOPE_DIM] = (q[:, :QK_NOPE_DIM] * (SM_SCALE * LOG2E)).astype(BF16)
        q_ref[0, hd, :, QK_NOPE_DIM:] = _apply_rope(q[:, QK_NOPE_DIM:], table).astype(BF16)


def _queries(x, shift, scale, norm_w, w_dq, q_norm, w_uq_slots, table, ts):
    b, s, d = x.shape
    full = lambda a: pl.BlockSpec(a.shape, lambda i, t: (0,) * a.ndim)
    vec = pl.BlockSpec((1, 1, d), lambda i, t: (i, 0, 0))
    return pl.pallas_call(
        _q_kernel,
        grid=(b, s // ts),
        in_specs=[
            pl.BlockSpec((1, ts, d), lambda i, t: (i, t, 0)),
            vec, vec, full(norm_w), full(w_dq), full(q_norm), full(w_uq_slots),
            pl.BlockSpec((1, ts, LANES), lambda i, t: (i, t, 0)),
        ],
        out_specs=pl.BlockSpec((1, N_HEADS, ts, HEAD_SLOT), lambda i, t: (i, 0, t, 0)),
        out_shape=jax.ShapeDtypeStruct((b, N_HEADS, s, HEAD_SLOT), BF16),
        compiler_params=_params("parallel", "parallel"),
        name="queries",
    )(x, shift, scale, norm_w, w_dq, q_norm, w_uq_slots, table)


def _attn_kernel(q_ref, k_ref, v_ref, o_ref, m_scr, l_scr, acc_scr, *, tk):
    qi = pl.program_id(2)
    tq = q_ref.shape[2]
    q = q_ref[0, 0]
    m_scr[...] = jnp.full(m_scr.shape, NEG_BIG, F32)
    l_scr[...] = jnp.zeros(l_scr.shape, F32)
    acc_scr[...] = jnp.zeros(acc_scr.shape, F32)

    def tile(j, masked):
        start = pl.multiple_of(j * tk, tk)
        k = k_ref[0, 0, pl.ds(start, tk), :]
        v = v_ref[0, 0, pl.ds(start, tk), :]
        s = lax.dot_general(q, k, (((1,), (1,)), ((), ())), preferred_element_type=F32)
        if masked:
            r = lax.broadcasted_iota(jnp.int32, s.shape, 0)
            c = lax.broadcasted_iota(jnp.int32, s.shape, 1)
            s = jnp.where(c <= r, s, NEG_BIG)
        m_prev = m_scr[...]
        m_new = jnp.maximum(m_prev, jnp.max(s, axis=-1, keepdims=True))
        alpha = jnp.exp2(m_prev - m_new)
        p = jnp.exp2(s - m_new[:, :1])
        l_scr[...] = alpha * l_scr[...] + jnp.sum(p, axis=-1, keepdims=True)
        acc_scr[...] = alpha * acc_scr[...] + _dot(p.astype(BF16), v)
        m_scr[...] = m_new

    def body(j, carry):
        tile(j, masked=False)
        return carry

    lax.fori_loop(0, qi * (tq // tk), body, 0)
    tile(qi, masked=True)
    o_ref[0, 0] = (acc_scr[...] / l_scr[...]).astype(o_ref.dtype)


def _attention(q, k, v, tq):
    b, nh, s, _ = q.shape
    return pl.pallas_call(
        functools.partial(_attn_kernel, tk=tq),
        grid=(b, nh, s // tq),
        in_specs=[
            pl.BlockSpec((1, 1, tq, HEAD_SLOT), lambda i, h, t: (i, h, t, 0)),
            pl.BlockSpec((1, 1, s, HEAD_SLOT), lambda i, h, t: (i, h, 0, 0)),
            pl.BlockSpec((1, 1, s, V_HEAD_DIM), lambda i, h, t: (i, h, 0, 0)),
        ],
        out_specs=pl.BlockSpec((1, 1, tq, V_HEAD_DIM), lambda i, h, t: (i, h, t, 0)),
        out_shape=jax.ShapeDtypeStruct((b, nh, s, V_HEAD_DIM), BF16),
        scratch_shapes=[
            pltpu.VMEM((tq, LANES), F32),
            pltpu.VMEM((tq, LANES), F32),
            pltpu.VMEM((tq, V_HEAD_DIM), F32),
        ],
        compiler_params=_params("parallel", "parallel", "parallel"),
        name="attention",
    )(q, k, v)


def _oproj_kernel(o_ref, x_ref, gate_ref, wo_ref, out_ref, cat_scr):
    for hd in range(N_HEADS):
        cat_scr[:, hd * V_HEAD_DIM:(hd + 1) * V_HEAD_DIM] = o_ref[0, hd]
    out_ref[0] = x_ref[0] + gate_ref[0] * _dot(cat_scr[...], wo_ref[...])


def _out_proj(o, x, gate, w_o, tm):
    b, s, d = x.shape
    return pl.pallas_call(
        _oproj_kernel,
        grid=(b, s // tm),
        in_specs=[
            pl.BlockSpec((1, N_HEADS, tm, V_HEAD_DIM), lambda i, t: (i, 0, t, 0)),
            pl.BlockSpec((1, tm, d), lambda i, t: (i, t, 0)),
            pl.BlockSpec((1, 1, d), lambda i, t: (i, 0, 0)),
            pl.BlockSpec(w_o.shape, lambda i, t: (0, 0)),
        ],
        out_specs=pl.BlockSpec((1, tm, d), lambda i, t: (i, t, 0)),
        out_shape=jax.ShapeDtypeStruct((b, s, d), F32),
        scratch_shapes=[pltpu.VMEM((tm, N_HEADS * V_HEAD_DIM), BF16)],
        compiler_params=_params("parallel", "parallel"),
        name="out_proj",
    )(o, x, gate, w_o)


ROW_TILE = 512
FFN_TILE = 512
MOD_TILE = 1024


def _swap_halves(w):
    half = w.shape[-1] // 2
    return jnp.concatenate([w[..., half:], w[..., :half]], axis=-1)


def _query_slots(w_uq):
    r = w_uq.shape[0]
    w = w_uq.reshape(r, N_HEADS, Q_HEAD_DIM)
    rope = w[..., QK_NOPE_DIM:]
    return jnp.concatenate([w[..., :QK_NOPE_DIM], rope, _swap_halves(rope)], axis=-1).reshape(
        r, N_HEADS * HEAD_SLOT)


def kernel(x, c, positions, mod_w, mod_b, norm_mix, norm_ffn, pool_w, pool_scale, kv_mod_w, kv_mod_b,
           kv_in_norm, w_dkv, kv_norm, w_uk, w_uv, w_kr, w_dq, q_norm, w_uq, w_o, ffn_gate, ffn_up,
           ffn_down, final_norm):
    b, s, d = x.shape
    nf = FFN_DIM // FFN_TILE

    c_rows = jnp.zeros((8, d), F32).at[:b].set(c)
    mod = _modulation(c_rows, mod_w, mod_b[:, None, :], MOD_TILE)[:, :b]
    kv_mod = _modulation(c_rows, kv_mod_w[None], kv_mod_b[None, None, :], MOD_TILE)[0, :b]
    table = _rope_table(positions, ROW_TILE)

    def vecs(a, n):
        return [v[:, None, :] for v in jnp.split(a, n, axis=-1)]

    def tiled_cols(w):
        return w.astype(BF16).reshape(d, nf, FFN_TILE).transpose(1, 0, 2)

    kv = None
    for i in range(DEPTH):
        shift_m, scale_m, gate_m, shift_f, scale_f, gate_f = vecs(mod[i], N_MOD)
        if i == N_A:
            kv_shift, kv_scale = vecs(kv_mod, 2)
            w_kr2 = jnp.concatenate([w_kr, _swap_halves(w_kr)], axis=-1).astype(BF16)
            kv = _shared_kv(x, kv_shift, kv_scale, kv_in_norm[None], w_dkv.astype(BF16), kv_norm[None],
                            w_uk.astype(BF16), w_uv.astype(BF16), w_kr2, table, ROW_TILE)
        if i < N_A:
            x = _pool_layer(x, shift_m, scale_m, gate_m, norm_mix[i][None], pool_w[i].astype(BF16),
                            pool_scale[i][None], ROW_TILE)
        else:
            j = i - N_A
            q = _queries(x, shift_m, scale_m, norm_mix[i][None], w_dq[j].astype(BF16), q_norm[j][None],
                         _query_slots(w_uq[j]).astype(BF16), table, ROW_TILE)
            o = _attention(q, kv[0], kv[1], ROW_TILE)
            x = _out_proj(o, x, gate_m, w_o[j].astype(BF16), ROW_TILE)
        x = _ffn_layer(x, shift_f, scale_f, gate_f, norm_ffn[i][None], tiled_cols(ffn_gate[i]),
                       tiled_cols(ffn_up[i]), ffn_down[i].astype(BF16), final_norm[None], ROW_TILE,
                       final_norm=(i == DEPTH - 1))
    return x
```

```python
import functools
import math

import numpy as np
import jax
import jax.numpy as jnp
from jax import lax
from jax.experimental import pallas as pl
from jax.experimental.pallas import tpu as pltpu

D_MODEL = 2048
DEPTH = 4
N_A = DEPTH // 2
POOL_WINDOWS = (2, 4, 8, 16)
N_POOL_GROUPS = len(POOL_WINDOWS)
POOL_GROUP_DIM = D_MODEL // N_POOL_GROUPS
POOL_HALO = 16
QK_NOPE_DIM = 128
QK_ROPE_DIM = 64
V_HEAD_DIM = 128
N_HEADS = D_MODEL // V_HEAD_DIM
KV_LORA_RANK = D_MODEL // 4
Q_LORA_RANK = 512
Q_HEAD_DIM = QK_NOPE_DIM + QK_ROPE_DIM
HEAD_SLOT = 256
SM_SCALE = Q_HEAD_DIM ** -0.5
ROPE_THETA = 10000.0
FFN_DIM = 5632
N_MOD = 6
EPS = 1e-6
LOG2E = math.log2(math.e)
NEG_BIG = -1e30

LANES = 128
VMEM_LIMIT = 56 * 1024 * 1024

BF16 = jnp.bfloat16
F32 = jnp.float32


def _params(*semantics):
    return pltpu.CompilerParams(dimension_semantics=semantics, vmem_limit_bytes=VMEM_LIMIT)


def _dot(a, b):
    return jnp.dot(a, b, preferred_element_type=F32)


def _norm_mod(x, gain, shift):
    ms = jnp.mean(x * x, axis=-1, keepdims=True)
    return x * lax.rsqrt(ms + EPS) * gain + shift


def _rmsnorm(x, g):
    ms = jnp.mean(x * x, axis=-1, keepdims=True)
    return x * lax.rsqrt(ms + EPS) * g


def _mod_kernel(c_ref, w_ref, b_ref, o_ref):
    c = c_ref[...]
    sc = (c * jax.nn.sigmoid(c)).astype(BF16)
    o_ref[...] = _dot(sc, w_ref[...].astype(BF16)) + b_ref[...]


def _modulation(c_rows, w, b, tn):
    n_layers, d, n = w.shape
    rows = c_rows.shape[0]
    return pl.pallas_call(
        _mod_kernel,
        grid=(n_layers, n // tn),
        in_specs=[
            pl.BlockSpec((rows, d), lambda l, j: (0, 0)),
            pl.BlockSpec((None, d, tn), lambda l, j: (l, 0, j)),
            pl.BlockSpec((None, 1, tn), lambda l, j: (l, 0, j)),
        ],
        out_specs=pl.BlockSpec((None, rows, tn), lambda l, j: (l, 0, j)),
        out_shape=jax.ShapeDtypeStruct((n_layers, rows, n), F32),
        compiler_params=_params("parallel", "parallel"),
        name="modulation",
    )(c_rows, w, b)


def _rope_table_kernel(pos_ref, freq_ref, o_ref):
    ang = pos_ref[0].astype(F32) * freq_ref[...]
    lane = lax.broadcasted_iota(jnp.int32, ang.shape, 1)
    sin = jnp.sin(ang)
    o_ref[0] = jnp.where(lane < 64, jnp.cos(ang), jnp.where(lane < 96, -sin, sin))


def _rope_table(positions, ts):
    b, s = positions.shape
    half = QK_ROPE_DIM // 2
    inv_freq = 1.0 / (ROPE_THETA ** (np.arange(0, QK_ROPE_DIM, 2, dtype=np.float32) / QK_ROPE_DIM))
    freq = jnp.asarray(np.tile(inv_freq.astype(np.float32), LANES // half)[None, :])
    return pl.pallas_call(
        _rope_table_kernel,
        grid=(b, s // ts),
        in_specs=[
            pl.BlockSpec((1, ts, 1), lambda i, t: (i, t, 0)),
            pl.BlockSpec((1, LANES), lambda i, t: (0, 0)),
        ],
        out_specs=pl.BlockSpec((1, ts, LANES), lambda i, t: (i, t, 0)),
        out_shape=jax.ShapeDtypeStruct((b, s, LANES), F32),
        compiler_params=_params("parallel", "parallel"),
        name="rope_table",
    )(positions.reshape(b, s, 1), freq)


def _apply_rope(t2, table):
    prod = t2 * table
    return prod + pltpu.roll(prod, 64, axis=1)


def _pool_kernel(x_ref, halo_ref, shift_ref, scale_ref, gate_ref, g_ref, w_ref, ps_ref, o_ref):
    t = pl.program_id(1)
    ts = x_ref.shape[1]
    x = x_ref[0]
    gain = g_ref[...] * (1.0 + scale_ref[0])
    shift = shift_ref[0]
    h = _norm_mod(x, gain, shift)
    hh = _norm_mod(halo_ref[0], gain, shift) * (t > 0).astype(F32)
    hs = jnp.concatenate([hh, h], axis=0)
    row = t * ts + lax.broadcasted_iota(jnp.int32, (ts, POOL_GROUP_DIM), 0)
    out_scale = gate_ref[0] * ps_ref[...]
    for g, w in enumerate(POOL_WINDOWS):
        lo, hi = g * POOL_GROUP_DIM, (g + 1) * POOL_GROUP_DIM
        s = hs[:, lo:hi]
        k = 1
        while k < w:
            s = s + pltpu.roll(s, k, axis=0)
            k *= 2
        cnt = jnp.minimum(row + 1, w).astype(F32)
        d = s[POOL_HALO:, :] / cnt - h[:, lo:hi]
        y = _dot(d.astype(BF16), w_ref[g])
        o_ref[0, :, lo:hi] = x[:, lo:hi] + out_scale[:, lo:hi] * y


def _pool_layer(x, shift, scale, gate, norm_w, pool_w, pool_scale, ts):
    b, s, d = x.shape
    halo_blocks = ts // POOL_HALO
    vec = pl.BlockSpec((1, 1, d), lambda i, t: (i, 0, 0))
    row = pl.BlockSpec((1, d), lambda i, t: (0, 0))
    return pl.pallas_call(
        _pool_kernel,
        grid=(b, s // ts),
        in_specs=[
            pl.BlockSpec((1, ts, d), lambda i, t: (i, t, 0)),
            pl.BlockSpec((1, POOL_HALO, d), lambda i, t: (i, jnp.maximum(t * halo_blocks - 1, 0), 0)),
            vec, vec, vec, row,
            pl.BlockSpec((N_POOL_GROUPS, POOL_GROUP_DIM, POOL_GROUP_DIM), lambda i, t: (0, 0, 0)),
            row,
        ],
        out_specs=pl.BlockSpec((1, ts, d), lambda i, t: (i, t, 0)),
        out_shape=jax.ShapeDtypeStruct((b, s, d), F32),
        compiler_params=_params("parallel", "parallel"),
        name="pool_layer",
    )(x, x, shift, scale, gate, norm_w, pool_w, pool_scale)


def _ffn_kernel(x_ref, shift_ref, scale_ref, gate_ref, g_ref, wg_ref, wu_ref, wd_ref, fn_ref,
                o_ref, h_scr, acc_scr, *, final_norm):
    f = pl.program_id(2)

    @pl.when(f == 0)
    def _():
        gain = g_ref[...] * (1.0 + scale_ref[0])
        h_scr[...] = _norm_mod(x_ref[0], gain, shift_ref[0]).astype(BF16)

    h = h_scr[...]
    a = _dot(h, wg_ref[...])
    u = _dot(h, wu_ref[...])
    act = (a * jax.nn.sigmoid(a) * u).astype(BF16)
    part = _dot(act, wd_ref[...])

    @pl.when(f == 0)
    def _():
        acc_scr[...] = part

    @pl.when(f > 0)
    def _():
        acc_scr[...] += part

    @pl.when(f == pl.num_programs(2) - 1)
    def _():
        out = x_ref[0] + gate_ref[0] * acc_scr[...]
        if final_norm:
            out = _rmsnorm(out, fn_ref[...])
        o_ref[0] = out


def _ffn_layer(x, shift, scale, gate, norm_w, wg, wu, wd, final_w, tm, final_norm):
    b, s, d = x.shape
    nf, _, tf = wg.shape
    vec = pl.BlockSpec((1, 1, d), lambda i, t, f: (i, 0, 0))
    row = pl.BlockSpec((1, d), lambda i, t, f: (0, 0))
    return pl.pallas_call(
        functools.partial(_ffn_kernel, final_norm=final_norm),
        grid=(b, s // tm, nf),
        in_specs=[
            pl.BlockSpec((1, tm, d), lambda i, t, f: (i, t, 0)),
            vec, vec, vec, row,
            pl.BlockSpec((None, d, tf), lambda i, t, f: (f, 0, 0)),
            pl.BlockSpec((None, d, tf), lambda i, t, f: (f, 0, 0)),
            pl.BlockSpec((tf, d), lambda i, t, f: (f, 0)),
            row,
        ],
        out_specs=pl.BlockSpec((1, tm, d), lambda i, t, f: (i, t, 0)),
        out_shape=jax.ShapeDtypeStruct((b, s, d), F32),
        scratch_shapes=[pltpu.VMEM((tm, d), BF16), pltpu.VMEM((tm, d), F32)],
        compiler_params=_params("parallel", "parallel", "arbitrary"),
        name="ffn_layer",
    )(x, shift, scale, gate, norm_w, wg, wu, wd, final_w)


def _kv_kernel(x_ref, shift_ref, scale_ref, g_ref, wdkv_ref, kvn_ref, wuk_ref, wuv_ref, wkr_ref,
               tab_ref, k_ref, v_ref):
    gain = g_ref[...] * (1.0 + scale_ref[0])
    h = _norm_mod(x_ref[0], gain, shift_ref[0]).astype(BF16)
    ckv = _rmsnorm(_dot(h, wdkv_ref[...]), kvn_ref[...]).astype(BF16)
    kn = _dot(ckv, wuk_ref[...]).astype(BF16)
    v = _dot(ckv, wuv_ref[...]).astype(BF16)
    kr = _apply_rope(_dot(h, wkr_ref[...]), tab_ref[0])
    lane = lax.broadcasted_iota(jnp.int32, kr.shape, 1)
    kr = jnp.where(lane < QK_ROPE_DIM, kr, 0.0).astype(BF16)
    for hd in range(N_HEADS):
        k_ref[0, hd, :, :QK_NOPE_DIM] = kn[:, hd * QK_NOPE_DIM:(hd + 1) * QK_NOPE_DIM]
        k_ref[0, hd, :, QK_NOPE_DIM:] = kr
        v_ref[0, hd] = v[:, hd * V_HEAD_DIM:(hd + 1) * V_HEAD_DIM]


def _shared_kv(x, shift, scale, norm_w, w_dkv, kv_norm, w_uk, w_uv, w_kr2, table, ts):
    b, s, d = x.shape
    full = lambda a: pl.BlockSpec(a.shape, lambda i, t: (0,) * a.ndim)
    vec = pl.BlockSpec((1, 1, d), lambda i, t: (i, 0, 0))
    return pl.pallas_call(
        _kv_kernel,
        grid=(b, s // ts),
        in_specs=[
            pl.BlockSpec((1, ts, d), lambda i, t: (i, t, 0)),
            vec, vec, full(norm_w), full(w_dkv), full(kv_norm), full(w_uk), full(w_uv), full(w_kr2),
            pl.BlockSpec((1, ts, LANES), lambda i, t: (i, t, 0)),
        ],
        out_specs=[
            pl.BlockSpec((1, N_HEADS, ts, HEAD_SLOT), lambda i, t: (i, 0, t, 0)),
            pl.BlockSpec((1, N_HEADS, ts, V_HEAD_DIM), lambda i, t: (i, 0, t, 0)),
        ],
        out_shape=[
            jax.ShapeDtypeStruct((b, N_HEADS, s, HEAD_SLOT), BF16),
            jax.ShapeDtypeStruct((b, N_HEADS, s, V_HEAD_DIM), BF16),
        ],
        compiler_params=_params("parallel", "parallel"),
        name="shared_kv",
    )(x, shift, scale, norm_w, w_dkv, kv_norm, w_uk, w_uv, w_kr2, table)


def _q_kernel(x_ref, shift_ref, scale_ref, g_ref, wdq_ref, qn_ref, wuq_ref, tab_ref, q_ref):
    gain = g_ref[...] * (1.0 + scale_ref[0])
    h = _norm_mod(x_ref[0], gain, shift_ref[0]).astype(BF16)
    cq = _rmsnorm(_dot(h, wdq_ref[...]), qn_ref[...]).astype(BF16)
    table = tab_ref[0] * (SM_SCALE * LOG2E)
    for hd in range(N_HEADS):
        q = _dot(cq, wuq_ref[:, hd * HEAD_SLOT:(hd + 1) * HEAD_SLOT])
        q_ref[0, hd, :, :QK_NOPE_DIM] = (q[:, :QK_NOPE_DIM] * (SM_SCALE * LOG2E)).astype(BF16)
        q_ref[0, hd, :, QK_NOPE_DIM:] = _apply_rope(q[:, QK_NOPE_DIM:], table).astype(BF16)


def _queries(x, shift, scale, norm_w, w_dq, q_norm, w_uq_slots, table, ts):
    b, s, d = x.shape
    full = lambda a: pl.BlockSpec(a.shape, lambda i, t: (0,) * a.ndim)
    vec = pl.BlockSpec((1, 1, d), lambda i, t: (i, 0, 0))
    return pl.pallas_call(
        _q_kernel,
        grid=(b, s // ts),
        in_specs=[
            pl.BlockSpec((1, ts, d), lambda i, t: (i, t, 0)),
            vec, vec, full(norm_w), full(w_dq), full(q_norm), full(w_uq_slots),
            pl.BlockSpec((1, ts, LANES), lambda i, t: (i, t, 0)),
        ],
        out_specs=pl.BlockSpec((1, N_HEADS, ts, HEAD_SLOT), lambda i, t: (i, 0, t, 0)),
        out_shape=jax.ShapeDtypeStruct((b, N_HEADS, s, HEAD_SLOT), BF16),
        compiler_params=_params("parallel", "parallel"),
        name="queries",
    )(x, shift, scale, norm_w, w_dq, q_norm, w_uq_slots, table)


def _lane_fold(x, op):
    r = x[:, :LANES]
    for c in range(1, x.shape[1] // LANES):
        r = op(r, x[:, c * LANES:(c + 1) * LANES])
    return r


def _attn_kernel(q_ref, k_ref, v_ref, o_ref, s_scr, *, tq):
    seq = q_ref.shape[2]
    lane = lax.broadcasted_iota(jnp.int32, (tq, LANES), 1)
    ones_col = jnp.where(lane == 0, 1.0, 0.0).astype(BF16)
    for t in range(seq // tq):
        rows = slice(t * tq, (t + 1) * tq)
        q = q_ref[0, 0, rows, :]
        m_run = None
        for j in range(t + 1):
            k = k_ref[0, 0, j * tq:(j + 1) * tq, :]
            s = lax.dot_general(q, k, (((1,), (1,)), ((), ())), preferred_element_type=F32)
            if j == t:
                r = lax.broadcasted_iota(jnp.int32, s.shape, 0)
                c = lax.broadcasted_iota(jnp.int32, s.shape, 1)
                s = jnp.where(c <= r, s, NEG_BIG)
            s_scr[j] = s
            f = _lane_fold(s, jnp.maximum)
            m_run = f if m_run is None else jnp.maximum(m_run, f)
        mb = jnp.broadcast_to(jnp.max(m_run, axis=-1, keepdims=True), (tq, LANES))
        acc = None
        for j in range(t + 1):
            s = s_scr[j]
            ps = [jnp.exp2(s[:, c * LANES:(c + 1) * LANES] - mb) for c in range(tq // LANES)]
            vv = jnp.concatenate([v_ref[0, 0, j * tq:(j + 1) * tq, :], ones_col], axis=1)
            d = _dot(jnp.concatenate(ps, axis=1).astype(BF16), vv)
            acc = d if acc is None else acc + d
        l = acc[:, V_HEAD_DIM:V_HEAD_DIM + 1]
        o_ref[0, 0, rows, :] = (acc[:, :V_HEAD_DIM] / l).astype(o_ref.dtype)


def _attention(q, k, v, tq):
    b, nh, s, _ = q.shape
    qk_spec = pl.BlockSpec((1, 1, s, HEAD_SLOT), lambda i, h: (i, h, 0, 0))
    v_spec = pl.BlockSpec((1, 1, s, V_HEAD_DIM), lambda i, h: (i, h, 0, 0))
    return pl.pallas_call(
        functools.partial(_attn_kernel, tq=tq),
        grid=(b, nh),
        in_specs=[qk_spec, qk_spec, v_spec],
        out_specs=v_spec,
        out_shape=jax.ShapeDtypeStruct((b, nh, s, V_HEAD_DIM), BF16),
        scratch_shapes=[pltpu.VMEM((s // tq, tq, tq), F32)],
        compiler_params=_params("parallel", "parallel"),
        name="attention",
    )(q, k, v)


def _oproj_kernel(o_ref, x_ref, gate_ref, wo_ref, out_ref, cat_scr):
    for hd in range(N_HEADS):
        cat_scr[:, hd * V_HEAD_DIM:(hd + 1) * V_HEAD_DIM] = o_ref[0, hd]
    out_ref[0] = x_ref[0] + gate_ref[0] * _dot(cat_scr[...], wo_ref[...])


def _out_proj(o, x, gate, w_o, tm):
    b, s, d = x.shape
    return pl.pallas_call(
        _oproj_kernel,
        grid=(b, s // tm),
        in_specs=[
            pl.BlockSpec((1, N_HEADS, tm, V_HEAD_DIM), lambda i, t: (i, 0, t, 0)),
            pl.BlockSpec((1, tm, d), lambda i, t: (i, t, 0)),
            pl.BlockSpec((1, 1, d), lambda i, t: (i, 0, 0)),
            pl.BlockSpec(w_o.shape, lambda i, t: (0, 0)),
        ],
        out_specs=pl.BlockSpec((1, tm, d), lambda i, t: (i, t, 0)),
        out_shape=jax.ShapeDtypeStruct((b, s, d), F32),
        scratch_shapes=[pltpu.VMEM((tm, N_HEADS * V_HEAD_DIM), BF16)],
        compiler_params=_params("parallel", "parallel"),
        name="out_proj",
    )(o, x, gate, w_o)


ROW_TILE = 512
FFN_TILE = 512
MOD_TILE = 1024


def _swap_halves(w):
    half = w.shape[-1] // 2
    return jnp.concatenate([w[..., half:], w[..., :half]], axis=-1)


def _query_slots(w_uq):
    r = w_uq.shape[0]
    w = w_uq.reshape(r, N_HEADS, Q_HEAD_DIM)
    rope = w[..., QK_NOPE_DIM:]
    return jnp.concatenate([w[..., :QK_NOPE_DIM], rope, _swap_halves(rope)], axis=-1).reshape(
        r, N_HEADS * HEAD_SLOT)


def kernel(x, c, positions, mod_w, mod_b, norm_mix, norm_ffn, pool_w, pool_scale, kv_mod_w, kv_mod_b,
           kv_in_norm, w_dkv, kv_norm, w_uk, w_uv, w_kr, w_dq, q_norm, w_uq, w_o, ffn_gate, ffn_up,
           ffn_down, final_norm):
    b, s, d = x.shape
    nf = FFN_DIM // FFN_TILE

    c_rows = jnp.zeros((8, d), F32).at[:b].set(c)
    mod = _modulation(c_rows, mod_w, mod_b[:, None, :], MOD_TILE)[:, :b]
    kv_mod = _modulation(c_rows, kv_mod_w[None], kv_mod_b[None, None, :], MOD_TILE)[0, :b]
    table = _rope_table(positions, ROW_TILE)

    def vecs(a, n):
        return [v[:, None, :] for v in jnp.split(a, n, axis=-1)]

    def tiled_cols(w):
        return w.astype(BF16).reshape(d, nf, FFN_TILE).transpose(1, 0, 2)

    kv = None
    for i in range(DEPTH):
        shift_m, scale_m, gate_m, shift_f, scale_f, gate_f = vecs(mod[i], N_MOD)
        if i == N_A:
            kv_shift, kv_scale = vecs(kv_mod, 2)
            w_kr2 = jnp.concatenate([w_kr, _swap_halves(w_kr)], axis=-1).astype(BF16)
            kv = _shared_kv(x, kv_shift, kv_scale, kv_in_norm[None], w_dkv.astype(BF16), kv_norm[None],
                            w_uk.astype(BF16), w_uv.astype(BF16), w_kr2, table, ROW_TILE)
        if i < N_A:
            x = _pool_layer(x, shift_m, scale_m, gate_m, norm_mix[i][None], pool_w[i].astype(BF16),
                            pool_scale[i][None], ROW_TILE)
        else:
            j = i - N_A
            q = _queries(x, shift_m, scale_m, norm_mix[i][None], w_dq[j].astype(BF16), q_norm[j][None],
                         _query_slots(w_uq[j]).astype(BF16), table, ROW_TILE)
            o = _attention(q, kv[0], kv[1], ROW_TILE)
            x = _out_proj(o, x, gate_m, w_o[j].astype(BF16), ROW_TILE)
        x = _ffn_layer(x, shift_f, scale_f, gate_f, norm_ffn[i][None], tiled_cols(ffn_gate[i]),
                       tiled_cols(ffn_up[i]), ffn_down[i].astype(BF16), final_norm[None], ROW_TILE,
                       final_norm=(i == DEPTH - 1))
    return x
```

```python
import functools
import math

import numpy as np
import jax
import jax.numpy as jnp
from jax import lax
from jax.experimental import pallas as pl
from jax.experimental.pallas import tpu as pltpu

D_MODEL = 2048
DEPTH = 4
N_A = DEPTH // 2
POOL_WINDOWS = (2, 4, 8, 16)
N_POOL_GROUPS = len(POOL_WINDOWS)
POOL_GROUP_DIM = D_MODEL // N_POOL_GROUPS
POOL_HALO = 16
QK_NOPE_DIM = 128
QK_ROPE_DIM = 64
V_HEAD_DIM = 128
N_HEADS = D_MODEL // V_HEAD_DIM
KV_LORA_RANK = D_MODEL // 4
Q_LORA_RANK = 512
Q_HEAD_DIM = QK_NOPE_DIM + QK_ROPE_DIM
HEAD_SLOT = 256
SM_SCALE = Q_HEAD_DIM ** -0.5
ROPE_THETA = 10000.0
FFN_DIM = 5632
N_MOD = 6
EPS = 1e-6
LOG2E = math.log2(math.e)
NEG_BIG = -1e30

LANES = 128
VMEM_LIMIT = 56 * 1024 * 1024

BF16 = jnp.bfloat16
F32 = jnp.float32


def _params(*semantics):
    return pltpu.CompilerParams(dimension_semantics=semantics, vmem_limit_bytes=VMEM_LIMIT)


def _dot(a, b):
    return jnp.dot(a, b, preferred_element_type=F32)


def _norm_mod(x, gain, shift):
    ms = jnp.mean(x * x, axis=-1, keepdims=True)
    return x * lax.rsqrt(ms + EPS) * gain + shift


def _rmsnorm(x, g):
    ms = jnp.mean(x * x, axis=-1, keepdims=True)
    return x * lax.rsqrt(ms + EPS) * g


def _mod_kernel(c_ref, w_ref, b_ref, o_ref):
    k = pl.program_id(1)

    @pl.when(k == 0)
    def _():
        o_ref[...] = jnp.broadcast_to(b_ref[...], o_ref.shape)

    c = c_ref[...]
    sc = (c * jax.nn.sigmoid(c)).astype(BF16)
    o_ref[...] += _dot(sc, w_ref[...].astype(BF16))


def _modulation(c_rows, w, b, tk):
    n_layers, d, n = w.shape
    rows = c_rows.shape[0]
    return pl.pallas_call(
        _mod_kernel,
        grid=(n_layers, d // tk),
        in_specs=[
            pl.BlockSpec((rows, tk), lambda l, k: (0, k)),
            pl.BlockSpec((None, tk, n), lambda l, k: (l, k, 0)),
            pl.BlockSpec((None, 1, n), lambda l, k: (l, 0, 0)),
        ],
        out_specs=pl.BlockSpec((None, rows, n), lambda l, k: (l, 0, 0)),
        out_shape=jax.ShapeDtypeStruct((n_layers, rows, n), F32),
        compiler_params=_params("parallel", "arbitrary"),
        name="modulation",
    )(c_rows, w, b)


def _rope_table_kernel(pos_ref, freq_ref, o_ref):
    ang = pos_ref[0].astype(F32) * freq_ref[...]
    lane = lax.broadcasted_iota(jnp.int32, ang.shape, 1)
    sin = jnp.sin(ang)
    o_ref[0] = jnp.where(lane < 64, jnp.cos(ang), jnp.where(lane < 96, -sin, sin))


def _rope_table(positions, ts):
    b, s = positions.shape
    half = QK_ROPE_DIM // 2
    inv_freq = 1.0 / (ROPE_THETA ** (np.arange(0, QK_ROPE_DIM, 2, dtype=np.float32) / QK_ROPE_DIM))
    freq = jnp.asarray(np.tile(inv_freq.astype(np.float32), LANES // half)[None, :])
    return pl.pallas_call(
        _rope_table_kernel,
        grid=(b, s // ts),
        in_specs=[
            pl.BlockSpec((1, ts, 1), lambda i, t: (i, t, 0)),
            pl.BlockSpec((1, LANES), lambda i, t: (0, 0)),
        ],
        out_specs=pl.BlockSpec((1, ts, LANES), lambda i, t: (i, t, 0)),
        out_shape=jax.ShapeDtypeStruct((b, s, LANES), F32),
        compiler_params=_params("parallel", "parallel"),
        name="rope_table",
    )(positions.reshape(b, s, 1), freq)


def _apply_rope(t2, table):
    prod = t2 * table
    return prod + pltpu.roll(prod, 64, axis=1)


def _pool_kernel(x_ref, halo_ref, shift_ref, scale_ref, gate_ref, g_ref, w_ref, ps_ref, o_ref):
    t = pl.program_id(1)
    ts = x_ref.shape[1]
    x = x_ref[0]
    gain = g_ref[...] * (1.0 + scale_ref[0])
    shift = shift_ref[0]
    h = _norm_mod(x, gain, shift)
    hh = _norm_mod(halo_ref[0], gain, shift) * (t > 0).astype(F32)
    hs = jnp.concatenate([hh, h], axis=0)
    row = t * ts + lax.broadcasted_iota(jnp.int32, (ts, POOL_GROUP_DIM), 0)
    out_scale = gate_ref[0] * ps_ref[...]
    for g, w in enumerate(POOL_WINDOWS):
        lo, hi = g * POOL_GROUP_DIM, (g + 1) * POOL_GROUP_DIM
        s = hs[:, lo:hi]
        k = 1
        while k < w:
            s = s + pltpu.roll(s, k, axis=0)
            k *= 2
        cnt = jnp.minimum(row + 1, w).astype(F32)
        d = s[POOL_HALO:, :] / cnt - h[:, lo:hi]
        y = _dot(d.astype(BF16), w_ref[g])
        o_ref[0, :, lo:hi] = x[:, lo:hi] + out_scale[:, lo:hi] * y


def _pool_layer(x, shift, scale, gate, norm_w, pool_w, pool_scale, ts):
    b, s, d = x.shape
    halo_blocks = ts // POOL_HALO
    vec = pl.BlockSpec((1, 1, d), lambda i, t: (i, 0, 0))
    row = pl.BlockSpec((1, d), lambda i, t: (0, 0))
    return pl.pallas_call(
        _pool_kernel,
        grid=(b, s // ts),
        in_specs=[
            pl.BlockSpec((1, ts, d), lambda i, t: (i, t, 0)),
            pl.BlockSpec((1, POOL_HALO, d), lambda i, t: (i, jnp.maximum(t * halo_blocks - 1, 0), 0)),
            vec, vec, vec, row,
            pl.BlockSpec((N_POOL_GROUPS, POOL_GROUP_DIM, POOL_GROUP_DIM), lambda i, t: (0, 0, 0)),
            row,
        ],
        out_specs=pl.BlockSpec((1, ts, d), lambda i, t: (i, t, 0)),
        out_shape=jax.ShapeDtypeStruct((b, s, d), F32),
        compiler_params=_params("parallel", "parallel"),
        name="pool_layer",
    )(x, x, shift, scale, gate, norm_w, pool_w, pool_scale)


def _ffn_kernel(x_ref, shift_ref, scale_ref, gate_ref, g_ref, wg_ref, wu_ref, wd_ref, fn_ref,
                o_ref, h_scr, acc_scr, *, final_norm):
    f = pl.program_id(2)

    @pl.when(f == 0)
    def _():
        gain = g_ref[...] * (1.0 + scale_ref[0])
        h_scr[...] = _norm_mod(x_ref[0], gain, shift_ref[0]).astype(BF16)
        acc_scr[...] = jnp.zeros(acc_scr.shape, F32)

    h = h_scr[...]
    a = _dot(h, wg_ref[...])
    u = _dot(h, wu_ref[...])
    act = (a * jax.nn.sigmoid(a) * u).astype(BF16)
    acc_scr[...] += _dot(act, wd_ref[...])

    @pl.when(f == pl.num_programs(2) - 1)
    def _():
        out = x_ref[0] + gate_ref[0] * acc_scr[...]
        if final_norm:
            out = _rmsnorm(out, fn_ref[...])
        o_ref[0] = out


def _ffn_layer(x, shift, scale, gate, norm_w, wg, wu, wd, final_w, layer, tm, tf, final_norm):
    b, s, d = x.shape
    nf = wg.shape[2] // tf
    vec = pl.BlockSpec((1, 1, d), lambda i, t, f: (i, 0, 0))
    row = pl.BlockSpec((1, d), lambda i, t, f: (0, 0))
    return pl.pallas_call(
        functools.partial(_ffn_kernel, final_norm=final_norm),
        grid=(b, s // tm, nf),
        in_specs=[
            pl.BlockSpec((1, tm, d), lambda i, t, f: (i, t, 0)),
            vec, vec, vec, row,
            pl.BlockSpec((None, d, tf), lambda i, t, f: (layer, 0, f)),
            pl.BlockSpec((None, d, tf), lambda i, t, f: (layer, 0, f)),
            pl.BlockSpec((None, tf, d), lambda i, t, f: (layer, f, 0)),
            row,
        ],
        out_specs=pl.BlockSpec((1, tm, d), lambda i, t, f: (i, t, 0)),
        out_shape=jax.ShapeDtypeStruct((b, s, d), F32),
        scratch_shapes=[pltpu.VMEM((tm, d), BF16), pltpu.VMEM((tm, d), F32)],
        compiler_params=_params("parallel", "parallel", "arbitrary"),
        name="ffn_layer",
    )(x, shift, scale, gate, norm_w, wg, wu, wd, final_w)


def _kv_kernel(x_ref, shift_ref, scale_ref, g_ref, wdkv_ref, kvn_ref, wuk_ref, wuv_ref, wkr_ref,
               tab_ref, k_ref, v_ref):
    gain = g_ref[...] * (1.0 + scale_ref[0])
    h = _norm_mod(x_ref[0], gain, shift_ref[0]).astype(BF16)
    ckv = _rmsnorm(_dot(h, wdkv_ref[...]), kvn_ref[...]).astype(BF16)
    kn = _dot(ckv, wuk_ref[...]).astype(BF16)
    v = _dot(ckv, wuv_ref[...]).astype(BF16)
    kr = _apply_rope(_dot(h, wkr_ref[...]), tab_ref[0])
    lane = lax.broadcasted_iota(jnp.int32, kr.shape, 1)
    kr = jnp.where(lane < QK_ROPE_DIM, kr, 0.0).astype(BF16)
    for hd in range(N_HEADS):
        k_ref[0, hd, :, :QK_NOPE_DIM] = kn[:, hd * QK_NOPE_DIM:(hd + 1) * QK_NOPE_DIM]
        k_ref[0, hd, :, QK_NOPE_DIM:] = kr
        v_ref[0, hd] = v[:, hd * V_HEAD_DIM:(hd + 1) * V_HEAD_DIM]


def _shared_kv(x, shift, scale, norm_w, w_dkv, kv_norm, w_uk, w_uv, w_kr2, table, ts):
    b, s, d = x.shape
    full = lambda a: pl.BlockSpec(a.shape, lambda i, t: (0,) * a.ndim)
    vec = pl.BlockSpec((1, 1, d), lambda i, t: (i, 0, 0))
    return pl.pallas_call(
        _kv_kernel,
        grid=(b, s // ts),
        in_specs=[
            pl.BlockSpec((1, ts, d), lambda i, t: (i, t, 0)),
            vec, vec, full(norm_w), full(w_dkv), full(kv_norm), full(w_uk), full(w_uv), full(w_kr2),
            pl.BlockSpec((1, ts, LANES), lambda i, t: (i, t, 0)),
        ],
        out_specs=[
            pl.BlockSpec((1, N_HEADS, ts, HEAD_SLOT), lambda i, t: (i, 0, t, 0)),
            pl.BlockSpec((1, N_HEADS, ts, V_HEAD_DIM), lambda i, t: (i, 0, t, 0)),
        ],
        out_shape=[
            jax.ShapeDtypeStruct((b, N_HEADS, s, HEAD_SLOT), BF16),
            jax.ShapeDtypeStruct((b, N_HEADS, s, V_HEAD_DIM), BF16),
        ],
        compiler_params=_params("parallel", "parallel"),
        name="shared_kv",
    )(x, shift, scale, norm_w, w_dkv, kv_norm, w_uk, w_uv, w_kr2, table)


def _q_kernel(x_ref, shift_ref, scale_ref, g_ref, wdq_ref, qn_ref, wuq_ref, tab_ref, q_ref):
    gain = g_ref[...] * (1.0 + scale_ref[0])
    h = _norm_mod(x_ref[0], gain, shift_ref[0]).astype(BF16)
    cq = _rmsnorm(_dot(h, wdq_ref[...]), qn_ref[...]).astype(BF16)
    table = tab_ref[0] * (SM_SCALE * LOG2E)
    for hd in range(N_HEADS):
        q = _dot(cq, wuq_ref[:, hd * HEAD_SLOT:(hd + 1) * HEAD_SLOT])
        q_ref[0, hd, :, :QK_NOPE_DIM] = (q[:, :QK_NOPE_DIM] * (SM_SCALE * LOG2E)).astype(BF16)
        q_ref[0, hd, :, QK_NOPE_DIM:] = _apply_rope(q[:, QK_NOPE_DIM:], table).astype(BF16)


def _queries(x, shift, scale, norm_w, w_dq, q_norm, w_uq_slots, table, ts):
    b, s, d = x.shape
    full = lambda a: pl.BlockSpec(a.shape, lambda i, t: (0,) * a.ndim)
    vec = pl.BlockSpec((1, 1, d), lambda i, t: (i, 0, 0))
    return pl.pallas_call(
        _q_kernel,
        grid=(b, s // ts),
        in_specs=[
            pl.BlockSpec((1, ts, d), lambda i, t: (i, t, 0)),
            vec, vec, full(norm_w), full(w_dq), full(q_norm), full(w_uq_slots),
            pl.BlockSpec((1, ts, LANES), lambda i, t: (i, t, 0)),
        ],
        out_specs=pl.BlockSpec((1, N_HEADS, ts, HEAD_SLOT), lambda i, t: (i, 0, t, 0)),
        out_shape=jax.ShapeDtypeStruct((b, N_HEADS, s, HEAD_SLOT), BF16),
        compiler_params=_params("parallel", "parallel"),
        name="queries",
    )(x, shift, scale, norm_w, w_dq, q_norm, w_uq_slots, table)


def _lane_fold(x, op):
    r = x[:, :LANES]
    for c in range(1, x.shape[1] // LANES):
        r = op(r, x[:, c * LANES:(c + 1) * LANES])
    return r


def _attn_kernel(q_ref, k_ref, v_ref, o_ref, s_scr, *, tq):
    seq = q_ref.shape[2]
    tk = s_scr.shape[2]
    lane = lax.broadcasted_iota(jnp.int32, (tk, LANES), 1)
    ones_col = jnp.where(lane == 0, 1.0, 0.0).astype(BF16)
    for t in range(seq // tq):
        rows = slice(t * tq, (t + 1) * tq)
        q = q_ref[0, 0, rows, :]
        m_run = None
        n_kv = (t + 1) * (tq // tk)
        for j in range(n_kv):
            k = k_ref[0, 0, j * tk:(j + 1) * tk, :]
            s = lax.dot_general(q, k, (((1,), (1,)), ((), ())), preferred_element_type=F32)
            if (j + 1) * tk > t * tq:
                r = lax.broadcasted_iota(jnp.int32, s.shape, 0) + t * tq
                c = lax.broadcasted_iota(jnp.int32, s.shape, 1) + j * tk
                s = jnp.where(c <= r, s, NEG_BIG)
            s_scr[j] = s
            f = _lane_fold(s, jnp.maximum)
            m_run = f if m_run is None else jnp.maximum(m_run, f)
        mb = jnp.broadcast_to(jnp.max(m_run, axis=-1, keepdims=True), (tq, LANES))
        acc = None
        for j in range(n_kv):
            s = s_scr[j]
            ps = [jnp.exp2(s[:, c * LANES:(c + 1) * LANES] - mb) for c in range(tk // LANES)]
            vv = jnp.concatenate([v_ref[0, 0, j * tk:(j + 1) * tk, :], ones_col], axis=1)
            d = _dot(jnp.concatenate(ps, axis=1).astype(BF16), vv)
            acc = d if acc is None else acc + d
        l = acc[:, V_HEAD_DIM:V_HEAD_DIM + 1]
        o_ref[0, 0, rows, :] = (acc[:, :V_HEAD_DIM] / l).astype(o_ref.dtype)


def _attention(q, k, v, tq, tk):
    b, nh, s, _ = q.shape
    qk_spec = pl.BlockSpec((1, 1, s, HEAD_SLOT), lambda i, h: (i, h, 0, 0))
    v_spec = pl.BlockSpec((1, 1, s, V_HEAD_DIM), lambda i, h: (i, h, 0, 0))
    return pl.pallas_call(
        functools.partial(_attn_kernel, tq=tq),
        grid=(b, nh),
        in_specs=[qk_spec, qk_spec, v_spec],
        out_specs=v_spec,
        out_shape=jax.ShapeDtypeStruct((b, nh, s, V_HEAD_DIM), BF16),
        scratch_shapes=[pltpu.VMEM((s // tk, tq, tk), F32)],
        compiler_params=_params("parallel", "parallel"),
        name="attention",
    )(q, k, v)


def _oproj_kernel(o_ref, x_ref, gate_ref, wo_ref, out_ref, cat_scr):
    for hd in range(N_HEADS):
        cat_scr[:, hd * V_HEAD_DIM:(hd + 1) * V_HEAD_DIM] = o_ref[0, hd]
    out_ref[0] = x_ref[0] + gate_ref[0] * _dot(cat_scr[...], wo_ref[...])


def _out_proj(o, x, gate, w_o, tm):
    b, s, d = x.shape
    return pl.pallas_call(
        _oproj_kernel,
        grid=(b, s // tm),
        in_specs=[
            pl.BlockSpec((1, N_HEADS, tm, V_HEAD_DIM), lambda i, t: (i, 0, t, 0)),
            pl.BlockSpec((1, tm, d), lambda i, t: (i, t, 0)),
            pl.BlockSpec((1, 1, d), lambda i, t: (i, 0, 0)),
            pl.BlockSpec(w_o.shape, lambda i, t: (0, 0)),
        ],
        out_specs=pl.BlockSpec((1, tm, d), lambda i, t: (i, t, 0)),
        out_shape=jax.ShapeDtypeStruct((b, s, d), F32),
        scratch_shapes=[pltpu.VMEM((tm, N_HEADS * V_HEAD_DIM), BF16)],
        compiler_params=_params("parallel", "parallel"),
        name="out_proj",
    )(o, x, gate, w_o)


ROW_TILE = 512
FFN_TILE = 512
MOD_TILE = 256
ATTN_KV_TILE = 512


def _swap_halves(w):
    half = w.shape[-1] // 2
    return jnp.concatenate([w[..., half:], w[..., :half]], axis=-1)


def _query_slots(w_uq):
    r = w_uq.shape[0]
    w = w_uq.reshape(r, N_HEADS, Q_HEAD_DIM)
    rope = w[..., QK_NOPE_DIM:]
    return jnp.concatenate([w[..., :QK_NOPE_DIM], rope, _swap_halves(rope)], axis=-1).reshape(
        r, N_HEADS * HEAD_SLOT)


def kernel(x, c, positions, mod_w, mod_b, norm_mix, norm_ffn, pool_w, pool_scale, kv_mod_w, kv_mod_b,
           kv_in_norm, w_dkv, kv_norm, w_uk, w_uv, w_kr, w_dq, q_norm, w_uq, w_o, ffn_gate, ffn_up,
           ffn_down, final_norm):
    b, s, d = x.shape

    c_rows = jnp.zeros((8, d), F32).at[:b].set(c)
    mod = _modulation(c_rows, mod_w, mod_b[:, None, :], MOD_TILE)[:, :b]
    kv_mod = _modulation(c_rows, kv_mod_w[None], kv_mod_b[None, None, :], MOD_TILE)[0, :b]
    table = _rope_table(positions, ROW_TILE)

    def vecs(a, n):
        return [v[:, None, :] for v in jnp.split(a, n, axis=-1)]

    wg, wu, wd = ffn_gate.astype(BF16), ffn_up.astype(BF16), ffn_down.astype(BF16)

    kv = None
    for i in range(DEPTH):
        shift_m, scale_m, gate_m, shift_f, scale_f, gate_f = vecs(mod[i], N_MOD)
        if i == N_A:
            kv_shift, kv_scale = vecs(kv_mod, 2)
            w_kr2 = jnp.concatenate([w_kr, _swap_halves(w_kr)], axis=-1).astype(BF16)
            kv = _shared_kv(x, kv_shift, kv_scale, kv_in_norm[None], w_dkv.astype(BF16), kv_norm[None],
                            w_uk.astype(BF16), w_uv.astype(BF16), w_kr2, table, ROW_TILE)
        if i < N_A:
            x = _pool_layer(x, shift_m, scale_m, gate_m, norm_mix[i][None], pool_w[i].astype(BF16),
                            pool_scale[i][None], ROW_TILE)
        else:
            j = i - N_A
            q = _queries(x, shift_m, scale_m, norm_mix[i][None], w_dq[j].astype(BF16), q_norm[j][None],
                         _query_slots(w_uq[j]).astype(BF16), table, ROW_TILE)
            o = _attention(q, kv[0], kv[1], ROW_TILE, ATTN_KV_TILE)
            x = _out_proj(o, x, gate_m, w_o[j].astype(BF16), ROW_TILE)
        x = _ffn_layer(x, shift_f, scale_f, gate_f, norm_ffn[i][None], wg, wu, wd, final_norm[None],
                       layer=i, tm=ROW_TILE, tf=FFN_TILE, final_norm=(i == DEPTH - 1))
    return x
```

```python
import functools
import math

import numpy as np
import jax
import jax.numpy as jnp
from jax import lax
from jax.experimental import pallas as pl
from jax.experimental.pallas import tpu as pltpu

D_MODEL = 2048
DEPTH = 4
N_A = DEPTH // 2
POOL_WINDOWS = (2, 4, 8, 16)
N_POOL_GROUPS = len(POOL_WINDOWS)
POOL_GROUP_DIM = D_MODEL // N_POOL_GROUPS
POOL_HALO = 16
QK_NOPE_DIM = 128
QK_ROPE_DIM = 64
V_HEAD_DIM = 128
N_HEADS = D_MODEL // V_HEAD_DIM
KV_LORA_RANK = D_MODEL // 4
Q_LORA_RANK = 512
Q_HEAD_DIM = QK_NOPE_DIM + QK_ROPE_DIM
HEAD_SLOT = 256
SM_SCALE = Q_HEAD_DIM ** -0.5
ROPE_THETA = 10000.0
FFN_DIM = 5632
N_MOD = 6
EPS = 1e-6
LOG2E = math.log2(math.e)
NEG_BIG = -1e30

LANES = 128
VMEM_LIMIT = 56 * 1024 * 1024

BF16 = jnp.bfloat16
F32 = jnp.float32


def _params(*semantics):
    return pltpu.CompilerParams(dimension_semantics=semantics, vmem_limit_bytes=VMEM_LIMIT)


def _dot(a, b):
    return jnp.dot(a, b, preferred_element_type=F32)


def _norm_mod(x, gain, shift):
    ms = jnp.mean(x * x, axis=-1, keepdims=True)
    return x * lax.rsqrt(ms + EPS) * gain + shift


def _rmsnorm(x, g):
    ms = jnp.mean(x * x, axis=-1, keepdims=True)
    return x * lax.rsqrt(ms + EPS) * g


def _mod_kernel(c_ref, w_ref, b_ref, o_ref):
    k = pl.program_id(1)

    @pl.when(k == 0)
    def _():
        o_ref[...] = jnp.broadcast_to(b_ref[...], o_ref.shape)

    c = c_ref[...]
    sc = (c * jax.nn.sigmoid(c)).astype(BF16)
    o_ref[...] += _dot(sc, w_ref[...].astype(BF16))


def _modulation(c_rows, w, b, tk):
    n_layers, d, n = w.shape
    rows = c_rows.shape[0]
    return pl.pallas_call(
        _mod_kernel,
        grid=(n_layers, d // tk),
        in_specs=[
            pl.BlockSpec((rows, tk), lambda l, k: (0, k)),
            pl.BlockSpec((None, tk, n), lambda l, k: (l, k, 0)),
            pl.BlockSpec((None, 1, n), lambda l, k: (l, 0, 0)),
        ],
        out_specs=pl.BlockSpec((None, rows, n), lambda l, k: (l, 0, 0)),
        out_shape=jax.ShapeDtypeStruct((n_layers, rows, n), F32),
        compiler_params=_params("parallel", "arbitrary"),
        name="modulation",
    )(c_rows, w, b)


def _rope_table_kernel(pos_ref, freq_ref, o_ref):
    ang = pos_ref[0].astype(F32) * freq_ref[...]
    lane = lax.broadcasted_iota(jnp.int32, ang.shape, 1)
    sin = jnp.sin(ang)
    o_ref[0] = jnp.where(lane < 64, jnp.cos(ang), jnp.where(lane < 96, -sin, sin))


def _rope_table(positions, ts):
    b, s = positions.shape
    half = QK_ROPE_DIM // 2
    inv_freq = 1.0 / (ROPE_THETA ** (np.arange(0, QK_ROPE_DIM, 2, dtype=np.float32) / QK_ROPE_DIM))
    freq = jnp.asarray(np.tile(inv_freq.astype(np.float32), LANES // half)[None, :])
    return pl.pallas_call(
        _rope_table_kernel,
        grid=(b, s // ts),
        in_specs=[
            pl.BlockSpec((1, ts, 1), lambda i, t: (i, t, 0)),
            pl.BlockSpec((1, LANES), lambda i, t: (0, 0)),
        ],
        out_specs=pl.BlockSpec((1, ts, LANES), lambda i, t: (i, t, 0)),
        out_shape=jax.ShapeDtypeStruct((b, s, LANES), F32),
        compiler_params=_params("parallel", "parallel"),
        name="rope_table",
    )(positions.reshape(b, s, 1), freq)


def _apply_rope(t2, table):
    prod = t2 * table
    return prod + pltpu.roll(prod, 64, axis=1)


def _pool_kernel(x_ref, halo_ref, shift_ref, scale_ref, gate_ref, g_ref, w_ref, ps_ref, o_ref):
    t = pl.program_id(1)
    ts = x_ref.shape[1]
    x = x_ref[0]
    gain = g_ref[...] * (1.0 + scale_ref[0])
    shift = shift_ref[0]
    h = _norm_mod(x, gain, shift)
    hh = _norm_mod(halo_ref[0], gain, shift) * (t > 0).astype(F32)
    hs = jnp.concatenate([hh, h], axis=0)
    row = t * ts + lax.broadcasted_iota(jnp.int32, (ts, POOL_GROUP_DIM), 0)
    out_scale = gate_ref[0] * ps_ref[...]
    for g, w in enumerate(POOL_WINDOWS):
        lo, hi = g * POOL_GROUP_DIM, (g + 1) * POOL_GROUP_DIM
        s = hs[:, lo:hi]
        k = 1
        while k < w:
            s = s + pltpu.roll(s, k, axis=0)
            k *= 2
        cnt = jnp.minimum(row + 1, w).astype(F32)
        d = s[POOL_HALO:, :] / cnt - h[:, lo:hi]
        y = _dot(d.astype(BF16), w_ref[g])
        o_ref[0, :, lo:hi] = x[:, lo:hi] + out_scale[:, lo:hi] * y


def _pool_layer(x, shift, scale, gate, norm_w, pool_w, pool_scale, ts):
    b, s, d = x.shape
    halo_blocks = ts // POOL_HALO
    vec = pl.BlockSpec((1, 1, d), lambda i, t: (i, 0, 0))
    row = pl.BlockSpec((1, d), lambda i, t: (0, 0))
    return pl.pallas_call(
        _pool_kernel,
        grid=(b, s // ts),
        in_specs=[
            pl.BlockSpec((1, ts, d), lambda i, t: (i, t, 0)),
            pl.BlockSpec((1, POOL_HALO, d), lambda i, t: (i, jnp.maximum(t * halo_blocks - 1, 0), 0)),
            vec, vec, vec, row,
            pl.BlockSpec((N_POOL_GROUPS, POOL_GROUP_DIM, POOL_GROUP_DIM), lambda i, t: (0, 0, 0)),
            row,
        ],
        out_specs=pl.BlockSpec((1, ts, d), lambda i, t: (i, t, 0)),
        out_shape=jax.ShapeDtypeStruct((b, s, d), F32),
        compiler_params=_params("parallel", "parallel"),
        name="pool_layer",
    )(x, x, shift, scale, gate, norm_w, pool_w, pool_scale)


def _ffn_kernel(x_ref, shift_ref, scale_ref, gate_ref, g_ref, wg_ref, wu_ref, wd_ref, fn_ref,
                o_ref, h_scr, acc_scr, *, final_norm):
    f = pl.program_id(2)

    @pl.when(f == 0)
    def _():
        gain = g_ref[...] * (1.0 + scale_ref[0])
        h_scr[...] = _norm_mod(x_ref[0], gain, shift_ref[0]).astype(BF16)
        acc_scr[...] = jnp.zeros(acc_scr.shape, F32)

    h = h_scr[...]
    a = _dot(h, wg_ref[...])
    u = _dot(h, wu_ref[...])
    act = (a * jax.nn.sigmoid(a) * u).astype(BF16)
    acc_scr[...] += _dot(act, wd_ref[...])

    @pl.when(f == pl.num_programs(2) - 1)
    def _():
        out = x_ref[0] + gate_ref[0] * acc_scr[...]
        if final_norm:
            out = _rmsnorm(out, fn_ref[...])
        o_ref[0] = out


def _retile_kernel(w_ref, o_ref):
    tc = o_ref.shape[2]
    for j in range(o_ref.shape[0]):
        o_ref[j] = w_ref[:, j * tc:(j + 1) * tc].astype(o_ref.dtype)


def _cast_retile(w, tc, tr):
    n_layers, r, c = w.shape
    return pl.pallas_call(
        _retile_kernel,
        grid=(n_layers, r // tr),
        in_specs=[pl.BlockSpec((None, tr, c), lambda l, i: (l, i, 0))],
        out_specs=pl.BlockSpec((None, c // tc, tr, tc), lambda l, i: (l, 0, i, 0)),
        out_shape=jax.ShapeDtypeStruct((n_layers, c // tc, r, tc), BF16),
        compiler_params=_params("parallel", "parallel"),
        name="cast_retile",
    )(w)


def _ffn_layer(x, shift, scale, gate, norm_w, wg, wu, wd, final_w, layer, tm, final_norm):
    b, s, d = x.shape
    _, nf, _, tf = wg.shape
    vec = pl.BlockSpec((1, 1, d), lambda i, t, f: (i, 0, 0))
    row = pl.BlockSpec((1, d), lambda i, t, f: (0, 0))
    return pl.pallas_call(
        functools.partial(_ffn_kernel, final_norm=final_norm),
        grid=(b, s // tm, nf),
        in_specs=[
            pl.BlockSpec((1, tm, d), lambda i, t, f: (i, t, 0)),
            vec, vec, vec, row,
            pl.BlockSpec((None, None, d, tf), lambda i, t, f: (layer, f, 0, 0)),
            pl.BlockSpec((None, None, d, tf), lambda i, t, f: (layer, f, 0, 0)),
            pl.BlockSpec((None, tf, d), lambda i, t, f: (layer, f, 0)),
            row,
        ],
        out_specs=pl.BlockSpec((1, tm, d), lambda i, t, f: (i, t, 0)),
        out_shape=jax.ShapeDtypeStruct((b, s, d), F32),
        scratch_shapes=[pltpu.VMEM((tm, d), BF16), pltpu.VMEM((tm, d), F32)],
        compiler_params=_params("parallel", "parallel", "arbitrary"),
        name="ffn_layer",
    )(x, shift, scale, gate, norm_w, wg, wu, wd, final_w)


def _kv_kernel(x_ref, shift_ref, scale_ref, g_ref, wdkv_ref, kvn_ref, wuk_ref, wuv_ref, wkr_ref,
               tab_ref, k_ref, v_ref):
    gain = g_ref[...] * (1.0 + scale_ref[0])
    h = _norm_mod(x_ref[0], gain, shift_ref[0]).astype(BF16)
    ckv = _rmsnorm(_dot(h, wdkv_ref[...]), kvn_ref[...]).astype(BF16)
    kn = _dot(ckv, wuk_ref[...]).astype(BF16)
    v = _dot(ckv, wuv_ref[...]).astype(BF16)
    kr = _apply_rope(_dot(h, wkr_ref[...]), tab_ref[0])
    lane = lax.broadcasted_iota(jnp.int32, kr.shape, 1)
    kr = jnp.where(lane < QK_ROPE_DIM, kr, 0.0).astype(BF16)
    for hd in range(N_HEADS):
        k_ref[0, hd, :, :QK_NOPE_DIM] = kn[:, hd * QK_NOPE_DIM:(hd + 1) * QK_NOPE_DIM]
        k_ref[0, hd, :, QK_NOPE_DIM:] = kr
        v_ref[0, hd] = v[:, hd * V_HEAD_DIM:(hd + 1) * V_HEAD_DIM]


def _shared_kv(x, shift, scale, norm_w, w_dkv, kv_norm, w_uk, w_uv, w_kr2, table, ts):
    b, s, d = x.shape
    full = lambda a: pl.BlockSpec(a.shape, lambda i, t: (0,) * a.ndim)
    vec = pl.BlockSpec((1, 1, d), lambda i, t: (i, 0, 0))
    return pl.pallas_call(
        _kv_kernel,
        grid=(b, s // ts),
        in_specs=[
            pl.BlockSpec((1, ts, d), lambda i, t: (i, t, 0)),
            vec, vec, full(norm_w), full(w_dkv), full(kv_norm), full(w_uk), full(w_uv), full(w_kr2),
            pl.BlockSpec((1, ts, LANES), lambda i, t: (i, t, 0)),
        ],
        out_specs=[
            pl.BlockSpec((1, N_HEADS, ts, HEAD_SLOT), lambda i, t: (i, 0, t, 0)),
            pl.BlockSpec((1, N_HEADS, ts, V_HEAD_DIM), lambda i, t: (i, 0, t, 0)),
        ],
        out_shape=[
            jax.ShapeDtypeStruct((b, N_HEADS, s, HEAD_SLOT), BF16),
            jax.ShapeDtypeStruct((b, N_HEADS, s, V_HEAD_DIM), BF16),
        ],
        compiler_params=_params("parallel", "parallel"),
        name="shared_kv",
    )(x, shift, scale, norm_w, w_dkv, kv_norm, w_uk, w_uv, w_kr2, table)


def _q_kernel(x_ref, shift_ref, scale_ref, g_ref, wdq_ref, qn_ref, wuq_ref, tab_ref, q_ref):
    gain = g_ref[...] * (1.0 + scale_ref[0])
    h = _norm_mod(x_ref[0], gain, shift_ref[0]).astype(BF16)
    cq = _rmsnorm(_dot(h, wdq_ref[...]), qn_ref[...]).astype(BF16)
    table = tab_ref[0] * (SM_SCALE * LOG2E)
    for hd in range(N_HEADS):
        q = _dot(cq, wuq_ref[:, hd * HEAD_SLOT:(hd + 1) * HEAD_SLOT])
        q_ref[0, hd, :, :QK_NOPE_DIM] = (q[:, :QK_NOPE_DIM] * (SM_SCALE * LOG2E)).astype(BF16)
        q_ref[0, hd, :, QK_NOPE_DIM:] = _apply_rope(q[:, QK_NOPE_DIM:], table).astype(BF16)


def _queries(x, shift, scale, norm_w, w_dq, q_norm, w_uq_slots, table, ts):
    b, s, d = x.shape
    full = lambda a: pl.BlockSpec(a.shape, lambda i, t: (0,) * a.ndim)
    vec = pl.BlockSpec((1, 1, d), lambda i, t: (i, 0, 0))
    return pl.pallas_call(
        _q_kernel,
        grid=(b, s // ts),
        in_specs=[
            pl.BlockSpec((1, ts, d), lambda i, t: (i, t, 0)),
            vec, vec, full(norm_w), full(w_dq), full(q_norm), full(w_uq_slots),
            pl.BlockSpec((1, ts, LANES), lambda i, t: (i, t, 0)),
        ],
        out_specs=pl.BlockSpec((1, N_HEADS, ts, HEAD_SLOT), lambda i, t: (i, 0, t, 0)),
        out_shape=jax.ShapeDtypeStruct((b, N_HEADS, s, HEAD_SLOT), BF16),
        compiler_params=_params("parallel", "parallel"),
        name="queries",
    )(x, shift, scale, norm_w, w_dq, q_norm, w_uq_slots, table)


def _lane_fold(x, op):
    r = x[:, :LANES]
    for c in range(1, x.shape[1] // LANES):
        r = op(r, x[:, c * LANES:(c + 1) * LANES])
    return r


def _attn_kernel(q_ref, k_ref, v_ref, vprev_ref, o_ref, olast_ref, s_scr, slast_scr, mlast_scr, *, tq):
    seq = q_ref.shape[2]
    nq = seq // tq
    lane = lax.broadcasted_iota(jnp.int32, (tq, LANES), 1)
    ones_col = jnp.where(lane == 0, 1.0, 0.0).astype(BF16)

    def pv_pass(scores, mb, values, n_kv):
        acc = None
        for j in range(n_kv):
            s = scores[j]
            ps = [jnp.exp2(s[:, c * LANES:(c + 1) * LANES] - mb) for c in range(tq // LANES)]
            vv = jnp.concatenate([values[0, 0, j * tq:(j + 1) * tq, :], ones_col], axis=1)
            d = _dot(jnp.concatenate(ps, axis=1).astype(BF16), vv)
            acc = d if acc is None else acc + d
        return (acc[:, :V_HEAD_DIM] / acc[:, V_HEAD_DIM:V_HEAD_DIM + 1]).astype(o_ref.dtype)

    @pl.when(pl.program_id(0) == 0)
    def _():
        slast_scr[...] = jnp.zeros(slast_scr.shape, F32)
        mlast_scr[...] = jnp.zeros(mlast_scr.shape, F32)

    olast_ref[0, 0] = pv_pass(slast_scr, mlast_scr[...], vprev_ref, nq)

    for t in range(nq):
        rows = slice(t * tq, (t + 1) * tq)
        q = q_ref[0, 0, rows, :]
        scores = s_scr if t < nq - 1 else slast_scr
        m_run = None
        for j in range(t + 1):
            k = k_ref[0, 0, j * tq:(j + 1) * tq, :]
            s = lax.dot_general(q, k, (((1,), (1,)), ((), ())), preferred_element_type=F32)
            if j == t:
                r = lax.broadcasted_iota(jnp.int32, s.shape, 0)
                c = lax.broadcasted_iota(jnp.int32, s.shape, 1)
                s = jnp.where(c <= r, s, NEG_BIG)
            scores[j] = s
            f = _lane_fold(s, jnp.maximum)
            m_run = f if m_run is None else jnp.maximum(m_run, f)
        mb = jnp.broadcast_to(jnp.max(m_run, axis=-1, keepdims=True), (tq, LANES))
        if t < nq - 1:
            o_ref[0, 0, rows, :] = pv_pass(s_scr, mb, v_ref, t + 1)
        else:
            mlast_scr[...] = mb
            o_ref[0, 0, rows, :] = jnp.zeros((tq, V_HEAD_DIM), o_ref.dtype)


def _attention(q, k, v, tq):
    b, nh, s, _ = q.shape
    n_heads_total = b * nh

    def cur(n):
        m = jnp.minimum(n, n_heads_total - 1)
        return (m // nh, m % nh, 0, 0)

    def prev(n):
        m = jnp.maximum(n - 1, 0)
        return (m // nh, m % nh, 0, 0)

    return pl.pallas_call(
        functools.partial(_attn_kernel, tq=tq),
        grid=(n_heads_total + 1,),
        in_specs=[
            pl.BlockSpec((1, 1, s, HEAD_SLOT), cur),
            pl.BlockSpec((1, 1, s, HEAD_SLOT), cur),
            pl.BlockSpec((1, 1, s, V_HEAD_DIM), cur),
            pl.BlockSpec((1, 1, s, V_HEAD_DIM), prev),
        ],
        out_specs=[
            pl.BlockSpec((1, 1, s, V_HEAD_DIM), cur),
            pl.BlockSpec((1, 1, tq, V_HEAD_DIM), prev),
        ],
        out_shape=[
            jax.ShapeDtypeStruct((b, nh, s, V_HEAD_DIM), BF16),
            jax.ShapeDtypeStruct((b, nh, tq, V_HEAD_DIM), BF16),
        ],
        scratch_shapes=[
            pltpu.VMEM((s // tq - 1, tq, tq), F32),
            pltpu.VMEM((s // tq, tq, tq), F32),
            pltpu.VMEM((tq, LANES), F32),
        ],
        compiler_params=_params("arbitrary"),
        name="attention",
    )(q, k, v, v)


def _oproj_kernel(o_ref, olast_ref, x_ref, gate_ref, wo_ref, out_ref, cat_scr):
    last = pl.program_id(1) == pl.num_programs(1) - 1

    @pl.when(jnp.logical_not(last))
    def _():
        for hd in range(N_HEADS):
            cat_scr[:, hd * V_HEAD_DIM:(hd + 1) * V_HEAD_DIM] = o_ref[0, hd]

    @pl.when(last)
    def _():
        for hd in range(N_HEADS):
            cat_scr[:, hd * V_HEAD_DIM:(hd + 1) * V_HEAD_DIM] = olast_ref[0, hd]

    out_ref[0] = x_ref[0] + gate_ref[0] * _dot(cat_scr[...], wo_ref[...])


def _out_proj(o, o_last, x, gate, w_o):
    b, s, d = x.shape
    tm = o_last.shape[2]
    return pl.pallas_call(
        _oproj_kernel,
        grid=(b, s // tm),
        in_specs=[
            pl.BlockSpec((1, N_HEADS, tm, V_HEAD_DIM), lambda i, t: (i, 0, t, 0)),
            pl.BlockSpec((1, N_HEADS, tm, V_HEAD_DIM), lambda i, t: (i, 0, 0, 0)),
            pl.BlockSpec((1, tm, d), lambda i, t: (i, t, 0)),
            pl.BlockSpec((1, 1, d), lambda i, t: (i, 0, 0)),
            pl.BlockSpec(w_o.shape, lambda i, t: (0, 0)),
        ],
        out_specs=pl.BlockSpec((1, tm, d), lambda i, t: (i, t, 0)),
        out_shape=jax.ShapeDtypeStruct((b, s, d), F32),
        scratch_shapes=[pltpu.VMEM((tm, N_HEADS * V_HEAD_DIM), BF16)],
        compiler_params=_params("parallel", "parallel"),
        name="out_proj",
    )(o, o_last, x, gate, w_o)


ROW_TILE = 512
FFN_TILE = 512
MOD_TILE = 256
CAST_ROWS = 256


def _swap_halves(w):
    half = w.shape[-1] // 2
    return jnp.concatenate([w[..., half:], w[..., :half]], axis=-1)


def _query_slots(w_uq):
    r = w_uq.shape[0]
    w = w_uq.reshape(r, N_HEADS, Q_HEAD_DIM)
    rope = w[..., QK_NOPE_DIM:]
    return jnp.concatenate([w[..., :QK_NOPE_DIM], rope, _swap_halves(rope)], axis=-1).reshape(
        r, N_HEADS * HEAD_SLOT)


def kernel(x, c, positions, mod_w, mod_b, norm_mix, norm_ffn, pool_w, pool_scale, kv_mod_w, kv_mod_b,
           kv_in_norm, w_dkv, kv_norm, w_uk, w_uv, w_kr, w_dq, q_norm, w_uq, w_o, ffn_gate, ffn_up,
           ffn_down, final_norm):
    b, s, d = x.shape

    c_rows = jnp.zeros((8, d), F32).at[:b].set(c)
    mod = _modulation(c_rows, mod_w, mod_b[:, None, :], MOD_TILE)[:, :b]
    kv_mod = _modulation(c_rows, kv_mod_w[None], kv_mod_b[None, None, :], MOD_TILE)[0, :b]
    table = _rope_table(positions, ROW_TILE)

    def vecs(a, n):
        return [v[:, None, :] for v in jnp.split(a, n, axis=-1)]

    wg = _cast_retile(ffn_gate, FFN_TILE, CAST_ROWS)
    wu = _cast_retile(ffn_up, FFN_TILE, CAST_ROWS)
    wd = _cast_retile(ffn_down, d, CAST_ROWS).reshape(ffn_down.shape)

    kv = None
    for i in range(DEPTH):
        shift_m, scale_m, gate_m, shift_f, scale_f, gate_f = vecs(mod[i], N_MOD)
        if i == N_A:
            kv_shift, kv_scale = vecs(kv_mod, 2)
            w_kr2 = jnp.concatenate([w_kr, _swap_halves(w_kr)], axis=-1).astype(BF16)
            kv = _shared_kv(x, kv_shift, kv_scale, kv_in_norm[None], w_dkv.astype(BF16), kv_norm[None],
                            w_uk.astype(BF16), w_uv.astype(BF16), w_kr2, table, ROW_TILE)
        if i < N_A:
            x = _pool_layer(x, shift_m, scale_m, gate_m, norm_mix[i][None], pool_w[i].astype(BF16),
                            pool_scale[i][None], ROW_TILE)
        else:
            j = i - N_A
            q = _queries(x, shift_m, scale_m, norm_mix[i][None], w_dq[j].astype(BF16), q_norm[j][None],
                         _query_slots(w_uq[j]).astype(BF16), table, ROW_TILE)
            o, o_last = _attention(q, kv[0], kv[1], ROW_TILE)
            x = _out_proj(o, o_last, x, gate_m, w_o[j].astype(BF16))
        x = _ffn_layer(x, shift_f, scale_f, gate_f, norm_ffn[i][None], wg, wu, wd, final_norm[None],
                       layer=i, tm=ROW_TILE, final_norm=(i == DEPTH - 1))
    return x
```

```python
import functools
import math

import numpy as np
import jax
import jax.numpy as jnp
from jax import lax
from jax.experimental import pallas as pl
from jax.experimental.pallas import tpu as pltpu

D_MODEL = 2048
DEPTH = 4
N_A = DEPTH // 2
POOL_WINDOWS = (2, 4, 8, 16)
N_POOL_GROUPS = len(POOL_WINDOWS)
POOL_GROUP_DIM = D_MODEL // N_POOL_GROUPS
POOL_HALO = 16
QK_NOPE_DIM = 128
QK_ROPE_DIM = 64
V_HEAD_DIM = 128
N_HEADS = D_MODEL // V_HEAD_DIM
KV_LORA_RANK = D_MODEL // 4
Q_LORA_RANK = 512
Q_HEAD_DIM = QK_NOPE_DIM + QK_ROPE_DIM
HEAD_SLOT = 256
SM_SCALE = Q_HEAD_DIM ** -0.5
ROPE_THETA = 10000.0
FFN_DIM = 5632
N_MOD = 6
EPS = 1e-6
LOG2E = math.log2(math.e)
NEG_BIG = -1e30

LANES = 128
VMEM_LIMIT = 56 * 1024 * 1024

BF16 = jnp.bfloat16
F32 = jnp.float32


def _params(*semantics):
    return pltpu.CompilerParams(dimension_semantics=semantics, vmem_limit_bytes=VMEM_LIMIT)


def _dot(a, b):
    return jnp.dot(a, b, preferred_element_type=F32)


def _norm_mod(x, gain, shift):
    ms = jnp.mean(x * x, axis=-1, keepdims=True)
    return x * lax.rsqrt(ms + EPS) * gain + shift


def _rmsnorm(x, g):
    ms = jnp.mean(x * x, axis=-1, keepdims=True)
    return x * lax.rsqrt(ms + EPS) * g


def _mod_kernel(c_ref, w_ref, b_ref, o_ref):
    k = pl.program_id(1)

    @pl.when(k == 0)
    def _():
        o_ref[...] = jnp.broadcast_to(b_ref[...], o_ref.shape)

    c = c_ref[...]
    sc = (c * jax.nn.sigmoid(c)).astype(BF16)
    o_ref[...] += _dot(sc, w_ref[...].astype(BF16))


def _modulation(c_rows, w, b, tk):
    n_layers, d, n = w.shape
    rows = c_rows.shape[0]
    return pl.pallas_call(
        _mod_kernel,
        grid=(n_layers, d // tk),
        in_specs=[
            pl.BlockSpec((rows, tk), lambda l, k: (0, k)),
            pl.BlockSpec((None, tk, n), lambda l, k: (l, k, 0)),
            pl.BlockSpec((None, 1, n), lambda l, k: (l, 0, 0)),
        ],
        out_specs=pl.BlockSpec((None, rows, n), lambda l, k: (l, 0, 0)),
        out_shape=jax.ShapeDtypeStruct((n_layers, rows, n), F32),
        compiler_params=_params("parallel", "arbitrary"),
        name="modulation",
    )(c_rows, w, b)


def _rope_table_kernel(pos_ref, freq_ref, o_ref):
    ang = pos_ref[0].astype(F32) * freq_ref[...]
    lane = lax.broadcasted_iota(jnp.int32, ang.shape, 1)
    sin = jnp.sin(ang)
    o_ref[0] = jnp.where(lane < 64, jnp.cos(ang), jnp.where(lane < 96, -sin, sin))


def _rope_table(positions, ts):
    b, s = positions.shape
    half = QK_ROPE_DIM // 2
    inv_freq = 1.0 / (ROPE_THETA ** (np.arange(0, QK_ROPE_DIM, 2, dtype=np.float32) / QK_ROPE_DIM))
    freq = jnp.asarray(np.tile(inv_freq.astype(np.float32), LANES // half)[None, :])
    return pl.pallas_call(
        _rope_table_kernel,
        grid=(b, s // ts),
        in_specs=[
            pl.BlockSpec((1, ts, 1), lambda i, t: (i, t, 0)),
            pl.BlockSpec((1, LANES), lambda i, t: (0, 0)),
        ],
        out_specs=pl.BlockSpec((1, ts, LANES), lambda i, t: (i, t, 0)),
        out_shape=jax.ShapeDtypeStruct((b, s, LANES), F32),
        compiler_params=_params("parallel", "parallel"),
        name="rope_table",
    )(positions.reshape(b, s, 1), freq)


def _apply_rope(t2, table):
    prod = t2 * table
    return prod + pltpu.roll(prod, 64, axis=1)


def _pool_kernel(x_ref, halo_ref, shift_ref, scale_ref, gate_ref, g_ref, w_ref, ps_ref, o_ref):
    t = pl.program_id(1)
    ts = x_ref.shape[1]
    x = x_ref[0]
    gain = g_ref[...] * (1.0 + scale_ref[0])
    shift = shift_ref[0]
    h = _norm_mod(x, gain, shift)
    hh = _norm_mod(halo_ref[0], gain, shift) * (t > 0).astype(F32)
    hs = jnp.concatenate([hh, h], axis=0)
    row = t * ts + lax.broadcasted_iota(jnp.int32, (ts, POOL_GROUP_DIM), 0)
    out_scale = gate_ref[0] * ps_ref[...]
    for g, w in enumerate(POOL_WINDOWS):
        lo, hi = g * POOL_GROUP_DIM, (g + 1) * POOL_GROUP_DIM
        s = hs[:, lo:hi]
        k = 1
        while k < w:
            s = s + pltpu.roll(s, k, axis=0)
            k *= 2
        cnt = jnp.minimum(row + 1, w).astype(F32)
        d = s[POOL_HALO:, :] / cnt - h[:, lo:hi]
        y = _dot(d.astype(BF16), w_ref[g])
        o_ref[0, :, lo:hi] = x[:, lo:hi] + out_scale[:, lo:hi] * y


def _pool_layer(x, shift, scale, gate, norm_w, pool_w, pool_scale, ts):
    b, s, d = x.shape
    halo_blocks = ts // POOL_HALO
    vec = pl.BlockSpec((1, 1, d), lambda i, t: (i, 0, 0))
    row = pl.BlockSpec((1, d), lambda i, t: (0, 0))
    return pl.pallas_call(
        _pool_kernel,
        grid=(b, s // ts),
        in_specs=[
            pl.BlockSpec((1, ts, d), lambda i, t: (i, t, 0)),
            pl.BlockSpec((1, POOL_HALO, d), lambda i, t: (i, jnp.maximum(t * halo_blocks - 1, 0), 0)),
            vec, vec, vec, row,
            pl.BlockSpec((N_POOL_GROUPS, POOL_GROUP_DIM, POOL_GROUP_DIM), lambda i, t: (0, 0, 0)),
            row,
        ],
        out_specs=pl.BlockSpec((1, ts, d), lambda i, t: (i, t, 0)),
        out_shape=jax.ShapeDtypeStruct((b, s, d), F32),
        compiler_params=_params("parallel", "parallel"),
        name="pool_layer",
    )(x, x, shift, scale, gate, norm_w, pool_w, pool_scale)


def _ffn_kernel(x_ref, shift_ref, scale_ref, gate_ref, g_ref, wg_ref, wu_ref, wd_ref, fn_ref,
                o_ref, h_scr, act_scr, acc_scr, *, final_norm):
    f = pl.program_id(2)
    nf = pl.num_programs(2) - 1

    def hidden_panel(h):
        a = _dot(h, wg_ref[...])
        u = _dot(h, wu_ref[...])
        return (a * jax.nn.sigmoid(a) * u).astype(BF16)

    @pl.when(f == 0)
    def _():
        gain = g_ref[...] * (1.0 + scale_ref[0])
        h = _norm_mod(x_ref[0], gain, shift_ref[0]).astype(BF16)
        h_scr[...] = h
        acc_scr[...] = jnp.zeros(acc_scr.shape, F32)
        act_scr[...] = hidden_panel(h)

    @pl.when(jnp.logical_and(f > 0, f < nf))
    def _():
        part = _dot(act_scr[...], wd_ref[...])
        act_scr[...] = hidden_panel(h_scr[...])
        acc_scr[...] += part

    @pl.when(f == nf)
    def _():
        out = x_ref[0] + gate_ref[0] * (acc_scr[...] + _dot(act_scr[...], wd_ref[...]))
        if final_norm:
            out = _rmsnorm(out, fn_ref[...])
        o_ref[0] = out


def _retile_kernel(w_ref, o_ref):
    tc = o_ref.shape[2]
    for j in range(o_ref.shape[0]):
        o_ref[j] = w_ref[:, j * tc:(j + 1) * tc].astype(o_ref.dtype)


def _cast_retile(w, tc, tr):
    n_layers, r, c = w.shape
    return pl.pallas_call(
        _retile_kernel,
        grid=(n_layers, r // tr),
        in_specs=[pl.BlockSpec((None, tr, c), lambda l, i: (l, i, 0))],
        out_specs=pl.BlockSpec((None, c // tc, tr, tc), lambda l, i: (l, 0, i, 0)),
        out_shape=jax.ShapeDtypeStruct((n_layers, c // tc, r, tc), BF16),
        compiler_params=_params("parallel", "parallel"),
        name="cast_retile",
    )(w)


def _ffn_layer(x, shift, scale, gate, norm_w, wg, wu, wd, final_w, layer, tm, final_norm):
    b, s, d = x.shape
    _, nf, _, tf = wg.shape
    vec = pl.BlockSpec((1, 1, d), lambda i, t, f: (i, 0, 0))
    row = pl.BlockSpec((1, d), lambda i, t, f: (0, 0))
    return pl.pallas_call(
        functools.partial(_ffn_kernel, final_norm=final_norm),
        grid=(b, s // tm, nf + 1),
        in_specs=[
            pl.BlockSpec((1, tm, d), lambda i, t, f: (i, t, 0)),
            vec, vec, vec, row,
            pl.BlockSpec((None, None, d, tf), lambda i, t, f: (layer, jnp.minimum(f, nf - 1), 0, 0)),
            pl.BlockSpec((None, None, d, tf), lambda i, t, f: (layer, jnp.minimum(f, nf - 1), 0, 0)),
            pl.BlockSpec((None, tf, d), lambda i, t, f: (layer, jnp.maximum(f - 1, 0), 0)),
            row,
        ],
        out_specs=pl.BlockSpec((1, tm, d), lambda i, t, f: (i, t, 0)),
        out_shape=jax.ShapeDtypeStruct((b, s, d), F32),
        scratch_shapes=[pltpu.VMEM((tm, d), BF16), pltpu.VMEM((tm, tf), BF16), pltpu.VMEM((tm, d), F32)],
        compiler_params=_params("parallel", "parallel", "arbitrary"),
        name="ffn_layer",
    )(x, shift, scale, gate, norm_w, wg, wu, wd, final_w)


SUB_ROWS = 256


def _kv_kernel(x_ref, shift_ref, scale_ref, g_ref, wdkv_ref, kvn_ref, wuk_ref, wuv_ref, wkr_ref,
               tab_ref, k_ref, v_ref):
    gain = g_ref[...] * (1.0 + scale_ref[0])
    shift = shift_ref[0]
    for r0 in range(0, x_ref.shape[1], SUB_ROWS):
        rows = slice(r0, r0 + SUB_ROWS)
        h = _norm_mod(x_ref[0, rows, :], gain, shift).astype(BF16)
        ckv = _rmsnorm(_dot(h, wdkv_ref[...]), kvn_ref[...]).astype(BF16)
        kn = _dot(ckv, wuk_ref[...]).astype(BF16)
        v = _dot(ckv, wuv_ref[...]).astype(BF16)
        kr = _apply_rope(_dot(h, wkr_ref[...]), tab_ref[0, rows, :])
        lane = lax.broadcasted_iota(jnp.int32, kr.shape, 1)
        kr = jnp.where(lane < QK_ROPE_DIM, kr, 0.0).astype(BF16)
        for hd in range(N_HEADS):
            k_ref[0, hd, rows, :QK_NOPE_DIM] = kn[:, hd * QK_NOPE_DIM:(hd + 1) * QK_NOPE_DIM]
            k_ref[0, hd, rows, QK_NOPE_DIM:] = kr
            v_ref[0, hd, rows, :] = v[:, hd * V_HEAD_DIM:(hd + 1) * V_HEAD_DIM]


def _shared_kv(x, shift, scale, norm_w, w_dkv, kv_norm, w_uk, w_uv, w_kr2, table, ts):
    b, s, d = x.shape
    full = lambda a: pl.BlockSpec(a.shape, lambda i, t: (0,) * a.ndim)
    vec = pl.BlockSpec((1, 1, d), lambda i, t: (i, 0, 0))
    return pl.pallas_call(
        _kv_kernel,
        grid=(b, s // ts),
        in_specs=[
            pl.BlockSpec((1, ts, d), lambda i, t: (i, t, 0)),
            vec, vec, full(norm_w), full(w_dkv), full(kv_norm), full(w_uk), full(w_uv), full(w_kr2),
            pl.BlockSpec((1, ts, LANES), lambda i, t: (i, t, 0)),
        ],
        out_specs=[
            pl.BlockSpec((1, N_HEADS, ts, HEAD_SLOT), lambda i, t: (i, 0, t, 0)),
            pl.BlockSpec((1, N_HEADS, ts, V_HEAD_DIM), lambda i, t: (i, 0, t, 0)),
        ],
        out_shape=[
            jax.ShapeDtypeStruct((b, N_HEADS, s, HEAD_SLOT), BF16),
            jax.ShapeDtypeStruct((b, N_HEADS, s, V_HEAD_DIM), BF16),
        ],
        compiler_params=_params("parallel", "parallel"),
        name="shared_kv",
    )(x, shift, scale, norm_w, w_dkv, kv_norm, w_uk, w_uv, w_kr2, table)


def _q_kernel(x_ref, shift_ref, scale_ref, g_ref, wdq_ref, qn_ref, wuq_ref, tab_ref, q_ref):
    gain = g_ref[...] * (1.0 + scale_ref[0])
    h = _norm_mod(x_ref[0], gain, shift_ref[0]).astype(BF16)
    cq = _rmsnorm(_dot(h, wdq_ref[...]), qn_ref[...]).astype(BF16)
    table = tab_ref[0] * (SM_SCALE * LOG2E)
    for hd in range(N_HEADS):
        q = _dot(cq, wuq_ref[:, hd * HEAD_SLOT:(hd + 1) * HEAD_SLOT])
        q_ref[0, hd, :, :QK_NOPE_DIM] = (q[:, :QK_NOPE_DIM] * (SM_SCALE * LOG2E)).astype(BF16)
        q_ref[0, hd, :, QK_NOPE_DIM:] = _apply_rope(q[:, QK_NOPE_DIM:], table).astype(BF16)


def _queries(x, shift, scale, norm_w, w_dq, q_norm, w_uq_slots, table, ts):
    b, s, d = x.shape
    full = lambda a: pl.BlockSpec(a.shape, lambda i, t: (0,) * a.ndim)
    vec = pl.BlockSpec((1, 1, d), lambda i, t: (i, 0, 0))
    return pl.pallas_call(
        _q_kernel,
        grid=(b, s // ts),
        in_specs=[
            pl.BlockSpec((1, ts, d), lambda i, t: (i, t, 0)),
            vec, vec, full(norm_w), full(w_dq), full(q_norm), full(w_uq_slots),
            pl.BlockSpec((1, ts, LANES), lambda i, t: (i, t, 0)),
        ],
        out_specs=pl.BlockSpec((1, N_HEADS, ts, HEAD_SLOT), lambda i, t: (i, 0, t, 0)),
        out_shape=jax.ShapeDtypeStruct((b, N_HEADS, s, HEAD_SLOT), BF16),
        compiler_params=_params("parallel", "parallel"),
        name="queries",
    )(x, shift, scale, norm_w, w_dq, q_norm, w_uq_slots, table)


def _lane_fold(x, op):
    r = x[:, :LANES]
    for c in range(1, x.shape[1] // LANES):
        r = op(r, x[:, c * LANES:(c + 1) * LANES])
    return r


def _attn_kernel(q_ref, k_ref, v_ref, vprev_ref, o_ref, olast_ref, s_scr, slast_scr, mlast_scr, *, tq):
    seq = q_ref.shape[2]
    nq = seq // tq
    lane = lax.broadcasted_iota(jnp.int32, (tq, LANES), 1)
    ones_col = jnp.where(lane == 0, 1.0, 0.0).astype(BF16)

    def pv_pass(scores, mb, values, n_kv):
        acc = None
        for j in range(n_kv):
            s = scores[j]
            ps = [jnp.exp2(s[:, c * LANES:(c + 1) * LANES] - mb) for c in range(tq // LANES)]
            vv = jnp.concatenate([values[0, 0, j * tq:(j + 1) * tq, :], ones_col], axis=1)
            d = _dot(jnp.concatenate(ps, axis=1).astype(BF16), vv)
            acc = d if acc is None else acc + d
        return (acc[:, :V_HEAD_DIM] / acc[:, V_HEAD_DIM:V_HEAD_DIM + 1]).astype(o_ref.dtype)

    @pl.when(pl.program_id(0) == 0)
    def _():
        slast_scr[...] = jnp.zeros(slast_scr.shape, F32)
        mlast_scr[...] = jnp.zeros(mlast_scr.shape, F32)

    olast_ref[0, 0] = pv_pass(slast_scr, mlast_scr[...], vprev_ref, nq)

    for t in range(nq):
        rows = slice(t * tq, (t + 1) * tq)
        q = q_ref[0, 0, rows, :]
        scores = s_scr if t < nq - 1 else slast_scr
        m_run = None
        for j in range(t + 1):
            k = k_ref[0, 0, j * tq:(j + 1) * tq, :]
            s = lax.dot_general(q, k, (((1,), (1,)), ((), ())), preferred_element_type=F32)
            if j == t:
                r = lax.broadcasted_iota(jnp.int32, s.shape, 0)
                c = lax.broadcasted_iota(jnp.int32, s.shape, 1)
                s = jnp.where(c <= r, s, NEG_BIG)
            scores[j] = s
            f = _lane_fold(s, jnp.maximum)
            m_run = f if m_run is None else jnp.maximum(m_run, f)
        mb = jnp.broadcast_to(jnp.max(m_run, axis=-1, keepdims=True), (tq, LANES))
        if t < nq - 1:
            o_ref[0, 0, rows, :] = pv_pass(s_scr, mb, v_ref, t + 1)
        else:
            mlast_scr[...] = mb
            o_ref[0, 0, rows, :] = jnp.zeros((tq, V_HEAD_DIM), o_ref.dtype)


def _attention(q, k, v, tq):
    b, nh, s, _ = q.shape
    n_heads_total = b * nh

    def cur(n):
        m = jnp.minimum(n, n_heads_total - 1)
        return (m // nh, m % nh, 0, 0)

    def prev(n):
        m = jnp.maximum(n - 1, 0)
        return (m // nh, m % nh, 0, 0)

    return pl.pallas_call(
        functools.partial(_attn_kernel, tq=tq),
        grid=(n_heads_total + 1,),
        in_specs=[
            pl.BlockSpec((1, 1, s, HEAD_SLOT), cur),
            pl.BlockSpec((1, 1, s, HEAD_SLOT), cur),
            pl.BlockSpec((1, 1, s, V_HEAD_DIM), cur),
            pl.BlockSpec((1, 1, s, V_HEAD_DIM), prev),
        ],
        out_specs=[
            pl.BlockSpec((1, 1, s, V_HEAD_DIM), cur),
            pl.BlockSpec((1, 1, tq, V_HEAD_DIM), prev),
        ],
        out_shape=[
            jax.ShapeDtypeStruct((b, nh, s, V_HEAD_DIM), BF16),
            jax.ShapeDtypeStruct((b, nh, tq, V_HEAD_DIM), BF16),
        ],
        scratch_shapes=[
            pltpu.VMEM((s // tq - 1, tq, tq), F32),
            pltpu.VMEM((s // tq, tq, tq), F32),
            pltpu.VMEM((tq, LANES), F32),
        ],
        compiler_params=_params("arbitrary"),
        name="attention",
    )(q, k, v, v)


def _oproj_kernel(o_ref, olast_ref, x_ref, gate_ref, wo_ref, out_ref, cat_scr):
    last = pl.program_id(1) == pl.num_programs(1) - 1
    for hd in range(N_HEADS):
        cat_scr[:, hd * V_HEAD_DIM:(hd + 1) * V_HEAD_DIM] = jnp.where(last, olast_ref[0, hd], o_ref[0, hd])
    out_ref[0] = x_ref[0] + gate_ref[0] * _dot(cat_scr[...], wo_ref[...])


def _out_proj(o, o_last, x, gate, w_o):
    b, s, d = x.shape
    tm = o_last.shape[2]
    return pl.pallas_call(
        _oproj_kernel,
        grid=(b, s // tm),
        in_specs=[
            pl.BlockSpec((1, N_HEADS, tm, V_HEAD_DIM), lambda i, t: (i, 0, t, 0)),
            pl.BlockSpec((1, N_HEADS, tm, V_HEAD_DIM), lambda i, t: (i, 0, 0, 0)),
            pl.BlockSpec((1, tm, d), lambda i, t: (i, t, 0)),
            pl.BlockSpec((1, 1, d), lambda i, t: (i, 0, 0)),
            pl.BlockSpec(w_o.shape, lambda i, t: (0, 0)),
        ],
        out_specs=pl.BlockSpec((1, tm, d), lambda i, t: (i, t, 0)),
        out_shape=jax.ShapeDtypeStruct((b, s, d), F32),
        scratch_shapes=[pltpu.VMEM((tm, N_HEADS * V_HEAD_DIM), BF16)],
        compiler_params=_params("parallel", "parallel"),
        name="out_proj",
    )(o, o_last, x, gate, w_o)


ROW_TILE = 512
FFN_TILE = 512
MOD_TILE = 256
CAST_ROWS = 256


def _swap_halves(w):
    half = w.shape[-1] // 2
    return jnp.concatenate([w[..., half:], w[..., :half]], axis=-1)


def _query_slots(w_uq):
    r = w_uq.shape[0]
    w = w_uq.reshape(r, N_HEADS, Q_HEAD_DIM)
    rope = w[..., QK_NOPE_DIM:]
    return jnp.concatenate([w[..., :QK_NOPE_DIM], rope, _swap_halves(rope)], axis=-1).reshape(
        r, N_HEADS * HEAD_SLOT)


def kernel(x, c, positions, mod_w, mod_b, norm_mix, norm_ffn, pool_w, pool_scale, kv_mod_w, kv_mod_b,
           kv_in_norm, w_dkv, kv_norm, w_uk, w_uv, w_kr, w_dq, q_norm, w_uq, w_o, ffn_gate, ffn_up,
           ffn_down, final_norm):
    b, s, d = x.shape

    c_rows = jnp.zeros((8, d), F32).at[:b].set(c)
    mod = _modulation(c_rows, mod_w, mod_b[:, None, :], MOD_TILE)[:, :b]
    kv_mod = _modulation(c_rows, kv_mod_w[None], kv_mod_b[None, None, :], MOD_TILE)[0, :b]
    table = _rope_table(positions, ROW_TILE)

    def vecs(a, n):
        return [v[:, None, :] for v in jnp.split(a, n, axis=-1)]

    wg = _cast_retile(ffn_gate, FFN_TILE, CAST_ROWS)
    wu = _cast_retile(ffn_up, FFN_TILE, CAST_ROWS)
    wd = _cast_retile(ffn_down, d, CAST_ROWS).reshape(ffn_down.shape)

    kv = None
    for i in range(DEPTH):
        shift_m, scale_m, gate_m, shift_f, scale_f, gate_f = vecs(mod[i], N_MOD)
        if i == N_A:
            kv_shift, kv_scale = vecs(kv_mod, 2)
            w_kr2 = jnp.concatenate([w_kr, _swap_halves(w_kr)], axis=-1).astype(BF16)
            kv = _shared_kv(x, kv_shift, kv_scale, kv_in_norm[None], w_dkv.astype(BF16), kv_norm[None],
                            w_uk.astype(BF16), w_uv.astype(BF16), w_kr2, table, ROW_TILE)
        if i < N_A:
            x = _pool_layer(x, shift_m, scale_m, gate_m, norm_mix[i][None], pool_w[i].astype(BF16),
                            pool_scale[i][None], ROW_TILE)
        else:
            j = i - N_A
            q = _queries(x, shift_m, scale_m, norm_mix[i][None], w_dq[j].astype(BF16), q_norm[j][None],
                         _query_slots(w_uq[j]).astype(BF16), table, ROW_TILE)
            o, o_last = _attention(q, kv[0], kv[1], ROW_TILE)
            x = _out_proj(o, o_last, x, gate_m, w_o[j].astype(BF16))
        x = _ffn_layer(x, shift_f, scale_f, gate_f, norm_ffn[i][None], wg, wu, wd, final_norm[None],
                       layer=i, tm=ROW_TILE, final_norm=(i == DEPTH - 1))
    return x
```

```python
import functools
import math

import numpy as np
import jax
import jax.numpy as jnp
from jax import lax
from jax.experimental import pallas as pl
from jax.experimental.pallas import tpu as pltpu

D_MODEL = 2048
DEPTH = 4
N_A = DEPTH // 2
POOL_WINDOWS = (2, 4, 8, 16)
N_POOL_GROUPS = len(POOL_WINDOWS)
POOL_GROUP_DIM = D_MODEL // N_POOL_GROUPS
POOL_HALO = 16
QK_NOPE_DIM = 128
QK_ROPE_DIM = 64
V_HEAD_DIM = 128
N_HEADS = D_MODEL // V_HEAD_DIM
KV_LORA_RANK = D_MODEL // 4
Q_LORA_RANK = 512
Q_HEAD_DIM = QK_NOPE_DIM + QK_ROPE_DIM
HEAD_SLOT = 256
SM_SCALE = Q_HEAD_DIM ** -0.5
ROPE_THETA = 10000.0
FFN_DIM = 5632
N_MOD = 6
EPS = 1e-6
LOG2E = math.log2(math.e)
NEG_BIG = -1e30

LANES = 128
VMEM_LIMIT = 60000 * 1024

BF16 = jnp.bfloat16
F32 = jnp.float32


def _params(*semantics):
    return pltpu.CompilerParams(dimension_semantics=semantics, vmem_limit_bytes=VMEM_LIMIT)


def _dot(a, b):
    return jnp.dot(a, b, preferred_element_type=F32)


def _norm_mod(x, gain, shift):
    ms = jnp.mean(x * x, axis=-1, keepdims=True)
    return x * lax.rsqrt(ms + EPS) * gain + shift


def _rmsnorm(x, g):
    ms = jnp.mean(x * x, axis=-1, keepdims=True)
    return x * lax.rsqrt(ms + EPS) * g


def _mod_kernel(c_ref, w_ref, b_ref, o_ref):
    k = pl.program_id(1)

    @pl.when(k == 0)
    def _():
        o_ref[...] = jnp.broadcast_to(b_ref[...], o_ref.shape)

    c = c_ref[...]
    sc = (c * jax.nn.sigmoid(c)).astype(BF16)
    o_ref[...] += _dot(sc, w_ref[...].astype(BF16))


def _modulation(c_rows, w, b, tk):
    n_layers, d, n = w.shape
    rows = c_rows.shape[0]
    return pl.pallas_call(
        _mod_kernel,
        grid=(n_layers, d // tk),
        in_specs=[
            pl.BlockSpec((rows, tk), lambda l, k: (0, k)),
            pl.BlockSpec((None, tk, n), lambda l, k: (l, k, 0)),
            pl.BlockSpec((None, 1, n), lambda l, k: (l, 0, 0)),
        ],
        out_specs=pl.BlockSpec((None, rows, n), lambda l, k: (l, 0, 0)),
        out_shape=jax.ShapeDtypeStruct((n_layers, rows, n), F32),
        compiler_params=_params("parallel", "arbitrary"),
        name="modulation",
    )(c_rows, w, b)


def _rope_table_kernel(pos_ref, freq_ref, o_ref):
    ang = pos_ref[0].astype(F32) * freq_ref[...]
    lane = lax.broadcasted_iota(jnp.int32, ang.shape, 1)
    sin = jnp.sin(ang)
    o_ref[0] = jnp.where(lane < 64, jnp.cos(ang), jnp.where(lane < 96, -sin, sin))


def _rope_table(positions, ts):
    b, s = positions.shape
    half = QK_ROPE_DIM // 2
    inv_freq = 1.0 / (ROPE_THETA ** (np.arange(0, QK_ROPE_DIM, 2, dtype=np.float32) / QK_ROPE_DIM))
    freq = jnp.asarray(np.tile(inv_freq.astype(np.float32), LANES // half)[None, :])
    return pl.pallas_call(
        _rope_table_kernel,
        grid=(b, s // ts),
        in_specs=[
            pl.BlockSpec((1, ts, 1), lambda i, t: (i, t, 0)),
            pl.BlockSpec((1, LANES), lambda i, t: (0, 0)),
        ],
        out_specs=pl.BlockSpec((1, ts, LANES), lambda i, t: (i, t, 0)),
        out_shape=jax.ShapeDtypeStruct((b, s, LANES), F32),
        compiler_params=_params("parallel", "parallel"),
        name="rope_table",
    )(positions.reshape(b, s, 1), freq)


def _apply_rope(t2, table):
    prod = t2 * table
    return prod + pltpu.roll(prod, 64, axis=1)


def _pool_kernel(x_ref, halo_ref, shift_ref, scale_ref, gate_ref, g_ref, w_ref, ps_ref, o_ref):
    t = pl.program_id(1)
    ts = x_ref.shape[1]
    x = x_ref[0]
    gain = g_ref[...] * (1.0 + scale_ref[0])
    shift = shift_ref[0]
    h = _norm_mod(x, gain, shift)
    hh = _norm_mod(halo_ref[0], gain, shift) * (t > 0).astype(F32)
    hs = jnp.concatenate([hh, h], axis=0)
    row = t * ts + lax.broadcasted_iota(jnp.int32, (ts, POOL_GROUP_DIM), 0)
    out_scale = gate_ref[0] * ps_ref[...]
    for g, w in enumerate(POOL_WINDOWS):
        lo, hi = g * POOL_GROUP_DIM, (g + 1) * POOL_GROUP_DIM
        s = hs[:, lo:hi]
        k = 1
        while k < w:
            s = s + pltpu.roll(s, k, axis=0)
            k *= 2
        cnt = jnp.minimum(row + 1, w).astype(F32)
        d = s[POOL_HALO:, :] / cnt - h[:, lo:hi]
        y = _dot(d.astype(BF16), w_ref[g])
        o_ref[0, :, lo:hi] = x[:, lo:hi] + out_scale[:, lo:hi] * y


def _pool_layer(x, shift, scale, gate, norm_w, pool_w, pool_scale, ts):
    b, s, d = x.shape
    halo_blocks = ts // POOL_HALO
    vec = pl.BlockSpec((1, 1, d), lambda i, t: (i, 0, 0))
    row = pl.BlockSpec((1, d), lambda i, t: (0, 0))
    return pl.pallas_call(
        _pool_kernel,
        grid=(b, s // ts),
        in_specs=[
            pl.BlockSpec((1, ts, d), lambda i, t: (i, t, 0)),
            pl.BlockSpec((1, POOL_HALO, d), lambda i, t: (i, jnp.maximum(t * halo_blocks - 1, 0), 0)),
            vec, vec, vec, row,
            pl.BlockSpec((N_POOL_GROUPS, POOL_GROUP_DIM, POOL_GROUP_DIM), lambda i, t: (0, 0, 0)),
            row,
        ],
        out_specs=pl.BlockSpec((1, ts, d), lambda i, t: (i, t, 0)),
        out_shape=jax.ShapeDtypeStruct((b, s, d), F32),
        compiler_params=_params("parallel", "parallel"),
        name="pool_layer",
    )(x, x, shift, scale, gate, norm_w, pool_w, pool_scale)


def _ffn_kernel(x_hbm, shift_ref, scale_ref, gate_ref, g_ref, wg_ref, wu_ref, wd_ref, fn_ref,
                o_ref, xbuf, h_scr, sem, *, final_norm):
    n, f = pl.program_id(0), pl.program_id(1)
    tm, d = o_ref.shape[1:]
    row_chunks = [slice(r, r + SUB_ROWS) for r in range(0, tm, SUB_ROWS)]

    def fetch(tile):
        return pltpu.make_async_copy(x_hbm.at[tile], xbuf, sem)

    @pl.when(jnp.logical_and(n == 0, f == 0))
    def _():
        fetch(0).start()

    @pl.when(f == 0)
    def _():
        fetch(n).wait()
        gain = g_ref[...] * (1.0 + scale_ref[0])
        for rows in row_chunks:
            x = xbuf[rows, :]
            h_scr[rows, :] = _norm_mod(x, gain, shift_ref[0]).astype(BF16)
            o_ref[0, rows, :] = x

    @pl.when(jnp.logical_and(f == 1, n + 1 < pl.num_programs(0)))
    def _():
        fetch(n + 1).start()

    h = h_scr[...]
    a = _dot(h, wg_ref[...])
    u = _dot(h, wu_ref[...])
    act = (a * jax.nn.sigmoid(a) * u).astype(BF16)
    gate = gate_ref[0]
    for c0 in range(0, d, FFN_DOWN_COLS):
        cols = slice(c0, c0 + FFN_DOWN_COLS)
        o_ref[0, :, cols] += gate[:, cols] * _dot(act, wd_ref[:, cols])

    if final_norm:
        @pl.when(f == pl.num_programs(1) - 1)
        def _():
            for rows in row_chunks:
                o_ref[0, rows, :] = _rmsnorm(o_ref[0, rows, :], fn_ref[...])


def _retile_kernel(w_ref, o_ref):
    tc = o_ref.shape[2]
    for j in range(o_ref.shape[0]):
        o_ref[j] = w_ref[:, j * tc:(j + 1) * tc].astype(o_ref.dtype)


def _cast_retile(w, tc, tr):
    n_layers, r, c = w.shape
    return pl.pallas_call(
        _retile_kernel,
        grid=(n_layers, r // tr),
        in_specs=[pl.BlockSpec((None, tr, c), lambda l, i: (l, i, 0))],
        out_specs=pl.BlockSpec((None, c // tc, tr, tc), lambda l, i: (l, 0, i, 0)),
        out_shape=jax.ShapeDtypeStruct((n_layers, c // tc, r, tc), BF16),
        compiler_params=_params("parallel", "parallel"),
        name="cast_retile",
    )(w)


def _ffn_layer(x, shift, scale, gate, norm_w, wg, wu, wd, final_w, layer, tm, final_norm):
    b, s, d = x.shape
    _, nf, _, tf = wg.shape
    nt = s // tm
    vec = pl.BlockSpec((1, 1, d), lambda n, f: (n // nt, 0, 0))
    row = pl.BlockSpec((1, d), lambda n, f: (0, 0))
    return pl.pallas_call(
        functools.partial(_ffn_kernel, final_norm=final_norm),
        grid=(b * nt, nf),
        in_specs=[
            pl.BlockSpec(memory_space=pl.ANY),
            vec, vec, vec, row,
            pl.BlockSpec((None, None, d, tf), lambda n, f: (layer, f, 0, 0)),
            pl.BlockSpec((None, None, d, tf), lambda n, f: (layer, f, 0, 0)),
            pl.BlockSpec((None, tf, d), lambda n, f: (layer, f, 0)),
            row,
        ],
        out_specs=pl.BlockSpec((1, tm, d), lambda n, f: (n // nt, n % nt, 0)),
        out_shape=jax.ShapeDtypeStruct((b, s, d), F32),
        scratch_shapes=[pltpu.VMEM((tm, d), F32), pltpu.VMEM((tm, d), BF16), pltpu.SemaphoreType.DMA(())],
        compiler_params=_params("arbitrary", "arbitrary"),
        name="ffn_layer",
    )(x.reshape(b * nt, tm, d), shift, scale, gate, norm_w, wg, wu, wd, final_w)


SUB_ROWS = 256


def _kv_kernel(x_ref, shift_ref, scale_ref, g_ref, wdkv_ref, kvn_ref, wuk_ref, wuv_ref, wkr_ref,
               tab_ref, k_ref, v_ref):
    gain = g_ref[...] * (1.0 + scale_ref[0])
    shift = shift_ref[0]
    for r0 in range(0, x_ref.shape[1], SUB_ROWS):
        rows = slice(r0, r0 + SUB_ROWS)
        h = _norm_mod(x_ref[0, rows, :], gain, shift).astype(BF16)
        ckv = _rmsnorm(_dot(h, wdkv_ref[...]), kvn_ref[...]).astype(BF16)
        kn = _dot(ckv, wuk_ref[...]).astype(BF16)
        v = _dot(ckv, wuv_ref[...]).astype(BF16)
        kr = _apply_rope(_dot(h, wkr_ref[...]), tab_ref[0, rows, :])
        lane = lax.broadcasted_iota(jnp.int32, kr.shape, 1)
        kr = jnp.where(lane < QK_ROPE_DIM, kr, 0.0).astype(BF16)
        for hd in range(N_HEADS):
            k_ref[0, hd, rows, :QK_NOPE_DIM] = kn[:, hd * QK_NOPE_DIM:(hd + 1) * QK_NOPE_DIM]
            k_ref[0, hd, rows, QK_NOPE_DIM:] = kr
            v_ref[0, hd, rows, :] = v[:, hd * V_HEAD_DIM:(hd + 1) * V_HEAD_DIM]


def _shared_kv(x, shift, scale, norm_w, w_dkv, kv_norm, w_uk, w_uv, w_kr2, table, ts):
    b, s, d = x.shape
    full = lambda a: pl.BlockSpec(a.shape, lambda i, t: (0,) * a.ndim)
    vec = pl.BlockSpec((1, 1, d), lambda i, t: (i, 0, 0))
    return pl.pallas_call(
        _kv_kernel,
        grid=(b, s // ts),
        in_specs=[
            pl.BlockSpec((1, ts, d), lambda i, t: (i, t, 0)),
            vec, vec, full(norm_w), full(w_dkv), full(kv_norm), full(w_uk), full(w_uv), full(w_kr2),
            pl.BlockSpec((1, ts, LANES), lambda i, t: (i, t, 0)),
        ],
        out_specs=[
            pl.BlockSpec((1, N_HEADS, ts, HEAD_SLOT), lambda i, t: (i, 0, t, 0)),
            pl.BlockSpec((1, N_HEADS, ts, V_HEAD_DIM), lambda i, t: (i, 0, t, 0)),
        ],
        out_shape=[
            jax.ShapeDtypeStruct((b, N_HEADS, s, HEAD_SLOT), BF16),
            jax.ShapeDtypeStruct((b, N_HEADS, s, V_HEAD_DIM), BF16),
        ],
        compiler_params=_params("parallel", "parallel"),
        name="shared_kv",
    )(x, shift, scale, norm_w, w_dkv, kv_norm, w_uk, w_uv, w_kr2, table)


def _q_kernel(x_ref, shift_ref, scale_ref, g_ref, wdq_ref, qn_ref, wuq_ref, tab_ref, q_ref):
    gain = g_ref[...] * (1.0 + scale_ref[0])
    h = _norm_mod(x_ref[0], gain, shift_ref[0]).astype(BF16)
    cq = _rmsnorm(_dot(h, wdq_ref[...]), qn_ref[...]).astype(BF16)
    table = tab_ref[0] * (SM_SCALE * LOG2E)
    for hd in range(N_HEADS):
        q = _dot(cq, wuq_ref[:, hd * HEAD_SLOT:(hd + 1) * HEAD_SLOT])
        q_ref[0, hd, :, :QK_NOPE_DIM] = (q[:, :QK_NOPE_DIM] * (SM_SCALE * LOG2E)).astype(BF16)
        q_ref[0, hd, :, QK_NOPE_DIM:] = _apply_rope(q[:, QK_NOPE_DIM:], table).astype(BF16)


def _queries(x, shift, scale, norm_w, w_dq, q_norm, w_uq_slots, table, ts):
    b, s, d = x.shape
    full = lambda a: pl.BlockSpec(a.shape, lambda i, t: (0,) * a.ndim)
    vec = pl.BlockSpec((1, 1, d), lambda i, t: (i, 0, 0))
    return pl.pallas_call(
        _q_kernel,
        grid=(b, s // ts),
        in_specs=[
            pl.BlockSpec((1, ts, d), lambda i, t: (i, t, 0)),
            vec, vec, full(norm_w), full(w_dq), full(q_norm), full(w_uq_slots),
            pl.BlockSpec((1, ts, LANES), lambda i, t: (i, t, 0)),
        ],
        out_specs=pl.BlockSpec((1, N_HEADS, ts, HEAD_SLOT), lambda i, t: (i, 0, t, 0)),
        out_shape=jax.ShapeDtypeStruct((b, N_HEADS, s, HEAD_SLOT), BF16),
        compiler_params=_params("parallel", "parallel"),
        name="queries",
    )(x, shift, scale, norm_w, w_dq, q_norm, w_uq_slots, table)


def _lane_fold(x, op):
    r = x[:, :LANES]
    for c in range(1, x.shape[1] // LANES):
        r = op(r, x[:, c * LANES:(c + 1) * LANES])
    return r


def _attn_kernel(q_ref, k_ref, v_ref, vprev_ref, o_ref, olast_ref, s_scr, slast_scr, mlast_scr, *, tq):
    seq = q_ref.shape[2]
    nq = seq // tq
    lane = lax.broadcasted_iota(jnp.int32, (tq, LANES), 1)
    ones_col = jnp.where(lane == 0, 1.0, 0.0).astype(BF16)

    def pv_pass(scores, mb, values, n_kv):
        acc = None
        for j in range(n_kv):
            s = scores[j]
            ps = [jnp.exp2(s[:, c * LANES:(c + 1) * LANES] - mb) for c in range(tq // LANES)]
            vv = jnp.concatenate([values[0, 0, j * tq:(j + 1) * tq, :], ones_col], axis=1)
            d = _dot(jnp.concatenate(ps, axis=1).astype(BF16), vv)
            acc = d if acc is None else acc + d
        return (acc[:, :V_HEAD_DIM] / acc[:, V_HEAD_DIM:V_HEAD_DIM + 1]).astype(o_ref.dtype)

    @pl.when(pl.program_id(0) == 0)
    def _():
        slast_scr[...] = jnp.zeros(slast_scr.shape, F32)
        mlast_scr[...] = jnp.zeros(mlast_scr.shape, F32)

    olast_ref[0, 0] = pv_pass(slast_scr, mlast_scr[...], vprev_ref, nq)

    for t in range(nq):
        rows = slice(t * tq, (t + 1) * tq)
        q = q_ref[0, 0, rows, :]
        scores = s_scr if t < nq - 1 else slast_scr
        m_run = None
        for j in range(t + 1):
            k = k_ref[0, 0, j * tq:(j + 1) * tq, :]
            s = lax.dot_general(q, k, (((1,), (1,)), ((), ())), preferred_element_type=F32)
            if j == t:
                r = lax.broadcasted_iota(jnp.int32, s.shape, 0)
                c = lax.broadcasted_iota(jnp.int32, s.shape, 1)
                s = jnp.where(c <= r, s, NEG_BIG)
            scores[j] = s
            f = _lane_fold(s, jnp.maximum)
            m_run = f if m_run is None else jnp.maximum(m_run, f)
        mb = jnp.broadcast_to(jnp.max(m_run, axis=-1, keepdims=True), (tq, LANES))
        if t < nq - 1:
            o_ref[0, 0, rows, :] = pv_pass(s_scr, mb, v_ref, t + 1)
        else:
            mlast_scr[...] = mb
            o_ref[0, 0, rows, :] = jnp.zeros((tq, V_HEAD_DIM), o_ref.dtype)


def _attention(q, k, v, tq):
    b, nh, s, _ = q.shape
    n_heads_total = b * nh

    def cur(n):
        m = jnp.minimum(n, n_heads_total - 1)
        return (m // nh, m % nh, 0, 0)

    def prev(n):
        m = jnp.maximum(n - 1, 0)
        return (m // nh, m % nh, 0, 0)

    return pl.pallas_call(
        functools.partial(_attn_kernel, tq=tq),
        grid=(n_heads_total + 1,),
        in_specs=[
            pl.BlockSpec((1, 1, s, HEAD_SLOT), cur),
            pl.BlockSpec((1, 1, s, HEAD_SLOT), cur),
            pl.BlockSpec((1, 1, s, V_HEAD_DIM), cur),
            pl.BlockSpec((1, 1, s, V_HEAD_DIM), prev),
        ],
        out_specs=[
            pl.BlockSpec((1, 1, s, V_HEAD_DIM), cur),
            pl.BlockSpec((1, 1, tq, V_HEAD_DIM), prev),
        ],
        out_shape=[
            jax.ShapeDtypeStruct((b, nh, s, V_HEAD_DIM), BF16),
            jax.ShapeDtypeStruct((b, nh, tq, V_HEAD_DIM), BF16),
        ],
        scratch_shapes=[
            pltpu.VMEM((s // tq - 1, tq, tq), F32),
            pltpu.VMEM((s // tq, tq, tq), F32),
            pltpu.VMEM((tq, LANES), F32),
        ],
        compiler_params=_params("arbitrary"),
        name="attention",
    )(q, k, v, v)


def _oproj_kernel(o_ref, olast_ref, x_ref, gate_ref, wo_ref, out_ref, cat_scr):
    last = pl.program_id(1) == pl.num_programs(1) - 1
    for hd in range(N_HEADS):
        cat_scr[:, hd * V_HEAD_DIM:(hd + 1) * V_HEAD_DIM] = jnp.where(last, olast_ref[0, hd], o_ref[0, hd])
    out_ref[0] = x_ref[0] + gate_ref[0] * _dot(cat_scr[...], wo_ref[...])


def _out_proj(o, o_last, x, gate, w_o):
    b, s, d = x.shape
    tm = o_last.shape[2]
    return pl.pallas_call(
        _oproj_kernel,
        grid=(b, s // tm),
        in_specs=[
            pl.BlockSpec((1, N_HEADS, tm, V_HEAD_DIM), lambda i, t: (i, 0, t, 0)),
            pl.BlockSpec((1, N_HEADS, tm, V_HEAD_DIM), lambda i, t: (i, 0, 0, 0)),
            pl.BlockSpec((1, tm, d), lambda i, t: (i, t, 0)),
            pl.BlockSpec((1, 1, d), lambda i, t: (i, 0, 0)),
            pl.BlockSpec(w_o.shape, lambda i, t: (0, 0)),
        ],
        out_specs=pl.BlockSpec((1, tm, d), lambda i, t: (i, t, 0)),
        out_shape=jax.ShapeDtypeStruct((b, s, d), F32),
        scratch_shapes=[pltpu.VMEM((tm, N_HEADS * V_HEAD_DIM), BF16)],
        compiler_params=_params("parallel", "parallel"),
        name="out_proj",
    )(o, o_last, x, gate, w_o)


ROW_TILE = 512
FFN_ROWS = 1024
FFN_DOWN_COLS = 512
FFN_TILE = 512
MOD_TILE = 256
CAST_ROWS = 256


def _swap_halves(w):
    half = w.shape[-1] // 2
    return jnp.concatenate([w[..., half:], w[..., :half]], axis=-1)


def _query_slots(w_uq):
    r = w_uq.shape[0]
    w = w_uq.reshape(r, N_HEADS, Q_HEAD_DIM)
    rope = w[..., QK_NOPE_DIM:]
    return jnp.concatenate([w[..., :QK_NOPE_DIM], rope, _swap_halves(rope)], axis=-1).reshape(
        r, N_HEADS * HEAD_SLOT)


def kernel(x, c, positions, mod_w, mod_b, norm_mix, norm_ffn, pool_w, pool_scale, kv_mod_w, kv_mod_b,
           kv_in_norm, w_dkv, kv_norm, w_uk, w_uv, w_kr, w_dq, q_norm, w_uq, w_o, ffn_gate, ffn_up,
           ffn_down, final_norm):
    b, s, d = x.shape

    c_rows = jnp.zeros((8, d), F32).at[:b].set(c)
    mod = _modulation(c_rows, mod_w, mod_b[:, None, :], MOD_TILE)[:, :b]
    kv_mod = _modulation(c_rows, kv_mod_w[None], kv_mod_b[None, None, :], MOD_TILE)[0, :b]
    table = _rope_table(positions, ROW_TILE)

    def vecs(a, n):
        return [v[:, None, :] for v in jnp.split(a, n, axis=-1)]

    wg = _cast_retile(ffn_gate, FFN_TILE, CAST_ROWS)
    wu = _cast_retile(ffn_up, FFN_TILE, CAST_ROWS)
    wd = _cast_retile(ffn_down, d, CAST_ROWS).reshape(ffn_down.shape)

    kv = None
    for i in range(DEPTH):
        shift_m, scale_m, gate_m, shift_f, scale_f, gate_f = vecs(mod[i], N_MOD)
        if i == N_A:
            kv_shift, kv_scale = vecs(kv_mod, 2)
            w_kr2 = jnp.concatenate([w_kr, _swap_halves(w_kr)], axis=-1).astype(BF16)
            kv = _shared_kv(x, kv_shift, kv_scale, kv_in_norm[None], w_dkv.astype(BF16), kv_norm[None],
                            w_uk.astype(BF16), w_uv.astype(BF16), w_kr2, table, ROW_TILE)
        if i < N_A:
            x = _pool_layer(x, shift_m, scale_m, gate_m, norm_mix[i][None], pool_w[i].astype(BF16),
                            pool_scale[i][None], ROW_TILE)
        else:
            j = i - N_A
            q = _queries(x, shift_m, scale_m, norm_mix[i][None], w_dq[j].astype(BF16), q_norm[j][None],
                         _query_slots(w_uq[j]).astype(BF16), table, ROW_TILE)
            o, o_last = _attention(q, kv[0], kv[1], ROW_TILE)
            x = _out_proj(o, o_last, x, gate_m, w_o[j].astype(BF16))
        x = _ffn_layer(x, shift_f, scale_f, gate_f, norm_ffn[i][None], wg, wu, wd, final_norm[None],
                       layer=i, tm=FFN_ROWS, final_norm=(i == DEPTH - 1))
    return x
```

```python
import functools
import math

import numpy as np
import jax
import jax.numpy as jnp
from jax import lax
from jax.experimental import pallas as pl
from jax.experimental.pallas import tpu as pltpu

D_MODEL = 2048
DEPTH = 4
N_A = DEPTH // 2
POOL_WINDOWS = (2, 4, 8, 16)
N_POOL_GROUPS = len(POOL_WINDOWS)
POOL_GROUP_DIM = D_MODEL // N_POOL_GROUPS
POOL_HALO = 16
QK_NOPE_DIM = 128
QK_ROPE_DIM = 64
V_HEAD_DIM = 128
N_HEADS = D_MODEL // V_HEAD_DIM
KV_LORA_RANK = D_MODEL // 4
Q_LORA_RANK = 512
Q_HEAD_DIM = QK_NOPE_DIM + QK_ROPE_DIM
HEAD_SLOT = 256
SM_SCALE = Q_HEAD_DIM ** -0.5
ROPE_THETA = 10000.0
FFN_DIM = 5632
N_MOD = 6
EPS = 1e-6
LOG2E = math.log2(math.e)
NEG_BIG = -1e30

LANES = 128
VMEM_LIMIT = 60000 * 1024

BF16 = jnp.bfloat16
F32 = jnp.float32


def _params(*semantics):
    return pltpu.CompilerParams(dimension_semantics=semantics, vmem_limit_bytes=VMEM_LIMIT)


def _dot(a, b):
    return jnp.dot(a, b, preferred_element_type=F32)


def _norm_mod(x, gain, shift):
    ms = jnp.mean(x * x, axis=-1, keepdims=True)
    return x * lax.rsqrt(ms + EPS) * gain + shift


def _rmsnorm(x, g):
    ms = jnp.mean(x * x, axis=-1, keepdims=True)
    return x * lax.rsqrt(ms + EPS) * g


def _mod_kernel(c_ref, w_ref, b_ref, o_ref):
    k = pl.program_id(1)

    @pl.when(k == 0)
    def _():
        o_ref[...] = jnp.broadcast_to(b_ref[...], o_ref.shape)

    c = c_ref[...]
    sc = (c * jax.nn.sigmoid(c)).astype(BF16)
    o_ref[...] += _dot(sc, w_ref[...].astype(BF16))


def _modulation(c_rows, w, b, tk):
    n_layers, d, n = w.shape
    rows = c_rows.shape[0]
    return pl.pallas_call(
        _mod_kernel,
        grid=(n_layers, d // tk),
        in_specs=[
            pl.BlockSpec((rows, tk), lambda l, k: (0, k)),
            pl.BlockSpec((None, tk, n), lambda l, k: (l, k, 0)),
            pl.BlockSpec((None, 1, n), lambda l, k: (l, 0, 0)),
        ],
        out_specs=pl.BlockSpec((None, rows, n), lambda l, k: (l, 0, 0)),
        out_shape=jax.ShapeDtypeStruct((n_layers, rows, n), F32),
        compiler_params=_params("parallel", "arbitrary"),
        name="modulation",
    )(c_rows, w, b)


def _rope_table_kernel(pos_ref, freq_ref, o_ref):
    ang = pos_ref[0].astype(F32) * freq_ref[...]
    lane = lax.broadcasted_iota(jnp.int32, ang.shape, 1)
    sin = jnp.sin(ang)
    o_ref[0] = jnp.where(lane < 64, jnp.cos(ang), jnp.where(lane < 96, -sin, sin))


def _rope_table(positions, ts):
    b, s = positions.shape
    half = QK_ROPE_DIM // 2
    inv_freq = 1.0 / (ROPE_THETA ** (np.arange(0, QK_ROPE_DIM, 2, dtype=np.float32) / QK_ROPE_DIM))
    freq = jnp.asarray(np.tile(inv_freq.astype(np.float32), LANES // half)[None, :])
    return pl.pallas_call(
        _rope_table_kernel,
        grid=(b, s // ts),
        in_specs=[
            pl.BlockSpec((1, ts, 1), lambda i, t: (i, t, 0)),
            pl.BlockSpec((1, LANES), lambda i, t: (0, 0)),
        ],
        out_specs=pl.BlockSpec((1, ts, LANES), lambda i, t: (i, t, 0)),
        out_shape=jax.ShapeDtypeStruct((b, s, LANES), F32),
        compiler_params=_params("parallel", "parallel"),
        name="rope_table",
    )(positions.reshape(b, s, 1), freq)


def _apply_rope(t2, table):
    prod = t2 * table
    return prod + pltpu.roll(prod, 64, axis=1)


def _pool_kernel(x_ref, halo_ref, shift_ref, scale_ref, gate_ref, g_ref, w_ref, ps_ref, o_ref):
    t = pl.program_id(1)
    ts = x_ref.shape[1]
    x = x_ref[0]
    gain = g_ref[...] * (1.0 + scale_ref[0])
    shift = shift_ref[0]
    h = _norm_mod(x, gain, shift)
    hh = _norm_mod(halo_ref[0], gain, shift) * (t > 0).astype(F32)
    hs = jnp.concatenate([hh, h], axis=0)
    row = t * ts + lax.broadcasted_iota(jnp.int32, (ts, POOL_GROUP_DIM), 0)
    out_scale = gate_ref[0] * ps_ref[...]
    for g, w in enumerate(POOL_WINDOWS):
        lo, hi = g * POOL_GROUP_DIM, (g + 1) * POOL_GROUP_DIM
        s = hs[:, lo:hi]
        k = 1
        while k < w:
            s = s + pltpu.roll(s, k, axis=0)
            k *= 2
        cnt = jnp.minimum(row + 1, w).astype(F32)
        d = s[POOL_HALO:, :] / cnt - h[:, lo:hi]
        y = _dot(d.astype(BF16), w_ref[g])
        o_ref[0, :, lo:hi] = x[:, lo:hi] + out_scale[:, lo:hi] * y


def _pool_layer(x, shift, scale, gate, norm_w, pool_w, pool_scale, ts):
    b, s, d = x.shape
    halo_blocks = ts // POOL_HALO
    vec = pl.BlockSpec((1, 1, d), lambda i, t: (i, 0, 0))
    row = pl.BlockSpec((1, d), lambda i, t: (0, 0))
    return pl.pallas_call(
        _pool_kernel,
        grid=(b, s // ts),
        in_specs=[
            pl.BlockSpec((1, ts, d), lambda i, t: (i, t, 0)),
            pl.BlockSpec((1, POOL_HALO, d), lambda i, t: (i, jnp.maximum(t * halo_blocks - 1, 0), 0)),
            vec, vec, vec, row,
            pl.BlockSpec((N_POOL_GROUPS, POOL_GROUP_DIM, POOL_GROUP_DIM), lambda i, t: (0, 0, 0)),
            row,
        ],
        out_specs=pl.BlockSpec((1, ts, d), lambda i, t: (i, t, 0)),
        out_shape=jax.ShapeDtypeStruct((b, s, d), F32),
        compiler_params=_params("parallel", "parallel"),
        name="pool_layer",
    )(x, x, shift, scale, gate, norm_w, pool_w, pool_scale)


def _ffn_kernel(x_hbm, shift_ref, scale_ref, gate_ref, g_ref, wg_ref, wu_ref, wd_ref, fn_ref,
                o_ref, xbuf, h_scr, sem, *, final_norm):
    n, f = pl.program_id(0), pl.program_id(1)
    tm, d = o_ref.shape[1:]
    row_chunks = [slice(r, r + SUB_ROWS) for r in range(0, tm, SUB_ROWS)]

    def fetch(tile):
        return pltpu.make_async_copy(x_hbm.at[tile], xbuf, sem)

    @pl.when(jnp.logical_and(n == 0, f == 0))
    def _():
        fetch(0).start()

    @pl.when(f == 0)
    def _():
        fetch(n).wait()
        gain = g_ref[...] * (1.0 + scale_ref[0])
        shift = shift_ref[0]

        for rows in row_chunks:
            x = xbuf[rows, :]
            h_scr[rows, :] = _norm_mod(x, gain, shift).astype(BF16)
            o_ref[0, rows, :] = x

    @pl.when(jnp.logical_and(f == 1, n + 1 < pl.num_programs(0)))
    def _():
        fetch(n + 1).start()

    h = h_scr[...]
    a = _dot(h, wg_ref[...])
    u = _dot(h, wu_ref[...])
    act = (a * jax.nn.sigmoid(a) * u).astype(BF16)
    gate = gate_ref[0]
    for c0 in range(0, d, FFN_DOWN_COLS):
        cols = slice(c0, c0 + FFN_DOWN_COLS)
        o_ref[0, :, cols] += gate[:, cols] * _dot(act, wd_ref[:, cols])

    if final_norm:
        @pl.when(f == pl.num_programs(1) - 1)
        def _():
            for rows in row_chunks:
                o_ref[0, rows, :] = _rmsnorm(o_ref[0, rows, :], fn_ref[...])


def _retile_kernel(w_ref, o_ref):
    tc = o_ref.shape[2]
    for j in range(o_ref.shape[0]):
        o_ref[j] = w_ref[:, j * tc:(j + 1) * tc].astype(o_ref.dtype)


def _cast_retile(w, tc, tr):
    n_layers, r, c = w.shape
    return pl.pallas_call(
        _retile_kernel,
        grid=(n_layers, r // tr),
        in_specs=[pl.BlockSpec((None, tr, c), lambda l, i: (l, i, 0))],
        out_specs=pl.BlockSpec((None, c // tc, tr, tc), lambda l, i: (l, 0, i, 0)),
        out_shape=jax.ShapeDtypeStruct((n_layers, c // tc, r, tc), BF16),
        compiler_params=_params("parallel", "parallel"),
        name="cast_retile",
    )(w)


def _ffn_layer(x, shift, scale, gate, norm_w, wg, wu, wd, final_w, layer, tm, final_norm):
    b, s, d = x.shape
    _, nf, _, tf = wg.shape
    nt = s // tm
    vec = pl.BlockSpec((1, 1, d), lambda n, f: (n // nt, 0, 0))
    row = pl.BlockSpec((1, d), lambda n, f: (0, 0))
    return pl.pallas_call(
        functools.partial(_ffn_kernel, final_norm=final_norm),
        grid=(b * nt, nf),
        in_specs=[
            pl.BlockSpec(memory_space=pl.ANY),
            vec, vec, vec, row,
            pl.BlockSpec((None, None, d, tf), lambda n, f: (layer, f, 0, 0)),
            pl.BlockSpec((None, None, d, tf), lambda n, f: (layer, f, 0, 0)),
            pl.BlockSpec((None, tf, d), lambda n, f: (layer, f, 0)),
            row,
        ],
        out_specs=pl.BlockSpec((1, tm, d), lambda n, f: (n // nt, n % nt, 0)),
        out_shape=jax.ShapeDtypeStruct((b, s, d), F32),
        scratch_shapes=[pltpu.VMEM((tm, d), F32), pltpu.VMEM((tm, d), BF16), pltpu.SemaphoreType.DMA(())],
        compiler_params=_params("arbitrary", "arbitrary"),
        name="ffn_layer",
    )(x.reshape(b * nt, tm, d), shift, scale, gate, norm_w, wg, wu, wd, final_w)


SUB_ROWS = 256


def _kv_kernel(x_ref, shift_ref, scale_ref, g_ref, wdkv_ref, kvn_ref, wuk_ref, wuv_ref, wkr_ref,
               tab_ref, k_ref, v_ref):
    gain = g_ref[...] * (1.0 + scale_ref[0])
    shift = shift_ref[0]
    for r0 in range(0, x_ref.shape[1], SUB_ROWS):
        rows = slice(r0, r0 + SUB_ROWS)
        h = _norm_mod(x_ref[0, rows, :], gain, shift).astype(BF16)
        ckv = _rmsnorm(_dot(h, wdkv_ref[...]), kvn_ref[...]).astype(BF16)
        kn = _dot(ckv, wuk_ref[...]).astype(BF16)
        v = _dot(ckv, wuv_ref[...]).astype(BF16)
        kr = _apply_rope(_dot(h, wkr_ref[...]), tab_ref[0, rows, :])
        lane = lax.broadcasted_iota(jnp.int32, kr.shape, 1)
        kr = jnp.where(lane < QK_ROPE_DIM, kr, 0.0).astype(BF16)
        for hd in range(N_HEADS):
            k_ref[0, hd, rows, :QK_NOPE_DIM] = kn[:, hd * QK_NOPE_DIM:(hd + 1) * QK_NOPE_DIM]
            k_ref[0, hd, rows, QK_NOPE_DIM:] = kr
            v_ref[0, hd, rows, :] = v[:, hd * V_HEAD_DIM:(hd + 1) * V_HEAD_DIM]


def _shared_kv(x, shift, scale, norm_w, w_dkv, kv_norm, w_uk, w_uv, w_kr2, table, ts):
    b, s, d = x.shape
    full = lambda a: pl.BlockSpec(a.shape, lambda i, t: (0,) * a.ndim)
    vec = pl.BlockSpec((1, 1, d), lambda i, t: (i, 0, 0))
    return pl.pallas_call(
        _kv_kernel,
        grid=(b, s // ts),
        in_specs=[
            pl.BlockSpec((1, ts, d), lambda i, t: (i, t, 0)),
            vec, vec, full(norm_w), full(w_dkv), full(kv_norm), full(w_uk), full(w_uv), full(w_kr2),
            pl.BlockSpec((1, ts, LANES), lambda i, t: (i, t, 0)),
        ],
        out_specs=[
            pl.BlockSpec((1, N_HEADS, ts, HEAD_SLOT), lambda i, t: (i, 0, t, 0)),
            pl.BlockSpec((1, N_HEADS, ts, V_HEAD_DIM), lambda i, t: (i, 0, t, 0)),
        ],
        out_shape=[
            jax.ShapeDtypeStruct((b, N_HEADS, s, HEAD_SLOT), BF16),
            jax.ShapeDtypeStruct((b, N_HEADS, s, V_HEAD_DIM), BF16),
        ],
        compiler_params=_params("parallel", "parallel"),
        name="shared_kv",
    )(x, shift, scale, norm_w, w_dkv, kv_norm, w_uk, w_uv, w_kr2, table)


def _q_kernel(x_ref, shift_ref, scale_ref, g_ref, wdq_ref, qn_ref, wuq_ref, tab_ref, q_ref):
    gain = g_ref[...] * (1.0 + scale_ref[0])
    h = _norm_mod(x_ref[0], gain, shift_ref[0]).astype(BF16)
    cq = _rmsnorm(_dot(h, wdq_ref[...]), qn_ref[...]).astype(BF16)
    table = tab_ref[0] * (SM_SCALE * LOG2E)
    for hd in range(N_HEADS):
        q = _dot(cq, wuq_ref[:, hd * HEAD_SLOT:(hd + 1) * HEAD_SLOT])
        q_ref[0, hd, :, :QK_NOPE_DIM] = (q[:, :QK_NOPE_DIM] * (SM_SCALE * LOG2E)).astype(BF16)
        q_ref[0, hd, :, QK_NOPE_DIM:] = _apply_rope(q[:, QK_NOPE_DIM:], table).astype(BF16)


def _queries(x, shift, scale, norm_w, w_dq, q_norm, w_uq_slots, table, ts):
    b, s, d = x.shape
    full = lambda a: pl.BlockSpec(a.shape, lambda i, t: (0,) * a.ndim)
    vec = pl.BlockSpec((1, 1, d), lambda i, t: (i, 0, 0))
    return pl.pallas_call(
        _q_kernel,
        grid=(b, s // ts),
        in_specs=[
            pl.BlockSpec((1, ts, d), lambda i, t: (i, t, 0)),
            vec, vec, full(norm_w), full(w_dq), full(q_norm), full(w_uq_slots),
            pl.BlockSpec((1, ts, LANES), lambda i, t: (i, t, 0)),
        ],
        out_specs=pl.BlockSpec((1, N_HEADS, ts, HEAD_SLOT), lambda i, t: (i, 0, t, 0)),
        out_shape=jax.ShapeDtypeStruct((b, N_HEADS, s, HEAD_SLOT), BF16),
        compiler_params=_params("parallel", "parallel"),
        name="queries",
    )(x, shift, scale, norm_w, w_dq, q_norm, w_uq_slots, table)


def _lane_fold(x, op):
    r = x[:, :LANES]
    for c in range(1, x.shape[1] // LANES):
        r = op(r, x[:, c * LANES:(c + 1) * LANES])
    return r


def _attn_kernel(q_ref, k_ref, v_ref, vprev_ref, o_ref, olast_ref, s_scr, slast_scr, mlast_scr, *, tq):
    seq = q_ref.shape[2]
    nq = seq // tq
    lane = lax.broadcasted_iota(jnp.int32, (tq, LANES), 1)
    ones_col = jnp.where(lane == 0, 1.0, 0.0).astype(BF16)

    half = tq // 2

    def probs(s, mb):
        ps = [jnp.exp2(s[:, c * LANES:(c + 1) * LANES] - mb) for c in range(s.shape[1] // LANES)]
        return jnp.concatenate(ps, axis=1).astype(BF16)

    def pv_pass(scores, mb, values, n_kv):
        acc = None
        for j in range(n_kv):
            vv = jnp.concatenate([values[0, 0, j * tq:(j + 1) * tq, :], ones_col], axis=1)
            if j < n_kv - 1:
                d = _dot(probs(scores[j], mb), vv)
            else:
                d = _dot(probs(scores[j, :, :half], mb), vv[:half])
                d_low = _dot(probs(scores[j, half:, half:], mb[half:]), vv[half:])
                d = jnp.concatenate([d[:half], d[half:] + d_low], axis=0)
            acc = d if acc is None else acc + d
        return (acc[:, :V_HEAD_DIM] / acc[:, V_HEAD_DIM:V_HEAD_DIM + 1]).astype(o_ref.dtype)

    def causal(s, row0):
        r = lax.broadcasted_iota(jnp.int32, s.shape, 0) + row0
        c = lax.broadcasted_iota(jnp.int32, s.shape, 1)
        return jnp.where(c <= r, s, NEG_BIG)

    def qk(q, k):
        return lax.dot_general(q, k, (((1,), (1,)), ((), ())), preferred_element_type=F32)

    @pl.when(pl.program_id(0) == 0)
    def _():
        slast_scr[...] = jnp.zeros(slast_scr.shape, F32)
        mlast_scr[...] = jnp.zeros(mlast_scr.shape, F32)

    olast_ref[0, 0] = pv_pass(slast_scr, mlast_scr[...], vprev_ref, nq)

    for t in range(nq):
        rows = slice(t * tq, (t + 1) * tq)
        q = q_ref[0, 0, rows, :]
        scores = s_scr if t < nq - 1 else slast_scr
        m_run = None
        for j in range(t):
            s = qk(q, k_ref[0, 0, j * tq:(j + 1) * tq, :])
            scores[j] = s
            f = _lane_fold(s, jnp.maximum)
            m_run = f if m_run is None else jnp.maximum(m_run, f)
        s_left = causal(qk(q, k_ref[0, 0, t * tq:t * tq + half, :]), 0)
        s_low = causal(qk(q[half:], k_ref[0, 0, t * tq + half:(t + 1) * tq, :]), 0)
        scores[t, :, :half] = s_left
        scores[t, half:, half:] = s_low
        f = _lane_fold(s_left, jnp.maximum)
        f = jnp.concatenate([f[:half], jnp.maximum(f[half:], _lane_fold(s_low, jnp.maximum))], axis=0)
        m_run = f if m_run is None else jnp.maximum(m_run, f)
        mb = jnp.broadcast_to(jnp.max(m_run, axis=-1, keepdims=True), (tq, LANES))
        if t < nq - 1:
            o_ref[0, 0, rows, :] = pv_pass(s_scr, mb, v_ref, t + 1)
        else:
            mlast_scr[...] = mb
            o_ref[0, 0, rows, :] = jnp.zeros((tq, V_HEAD_DIM), o_ref.dtype)


def _attention(q, k, v, tq):
    b, nh, s, _ = q.shape
    n_heads_total = b * nh

    def cur(n):
        m = jnp.minimum(n, n_heads_total - 1)
        return (m // nh, m % nh, 0, 0)

    def prev(n):
        m = jnp.maximum(n - 1, 0)
        return (m // nh, m % nh, 0, 0)

    return pl.pallas_call(
        functools.partial(_attn_kernel, tq=tq),
        grid=(n_heads_total + 1,),
        in_specs=[
            pl.BlockSpec((1, 1, s, HEAD_SLOT), cur),
            pl.BlockSpec((1, 1, s, HEAD_SLOT), cur),
            pl.BlockSpec((1, 1, s, V_HEAD_DIM), cur),
            pl.BlockSpec((1, 1, s, V_HEAD_DIM), prev),
        ],
        out_specs=[
            pl.BlockSpec((1, 1, s, V_HEAD_DIM), cur),
            pl.BlockSpec((1, 1, tq, V_HEAD_DIM), prev),
        ],
        out_shape=[
            jax.ShapeDtypeStruct((b, nh, s, V_HEAD_DIM), BF16),
            jax.ShapeDtypeStruct((b, nh, tq, V_HEAD_DIM), BF16),
        ],
        scratch_shapes=[
            pltpu.VMEM((s // tq - 1, tq, tq), F32),
            pltpu.VMEM((s // tq, tq, tq), F32),
            pltpu.VMEM((tq, LANES), F32),
        ],
        compiler_params=_params("arbitrary"),
        name="attention",
    )(q, k, v, v)


def _oproj_kernel(o_ref, olast_ref, x_ref, gate_ref, wo_ref, out_ref, cat_scr):
    last = pl.program_id(1) == pl.num_programs(1) - 1
    for hd in range(N_HEADS):
        cat_scr[:, hd * V_HEAD_DIM:(hd + 1) * V_HEAD_DIM] = jnp.where(last, olast_ref[0, hd], o_ref[0, hd])
    out_ref[0] = x_ref[0] + gate_ref[0] * _dot(cat_scr[...], wo_ref[...])


def _out_proj(o, o_last, x, gate, w_o):
    b, s, d = x.shape
    tm = o_last.shape[2]
    return pl.pallas_call(
        _oproj_kernel,
        grid=(b, s // tm),
        in_specs=[
            pl.BlockSpec((1, N_HEADS, tm, V_HEAD_DIM), lambda i, t: (i, 0, t, 0)),
            pl.BlockSpec((1, N_HEADS, tm, V_HEAD_DIM), lambda i, t: (i, 0, 0, 0)),
            pl.BlockSpec((1, tm, d), lambda i, t: (i, t, 0)),
            pl.BlockSpec((1, 1, d), lambda i, t: (i, 0, 0)),
            pl.BlockSpec(w_o.shape, lambda i, t: (0, 0)),
        ],
        out_specs=pl.BlockSpec((1, tm, d), lambda i, t: (i, t, 0)),
        out_shape=jax.ShapeDtypeStruct((b, s, d), F32),
        scratch_shapes=[pltpu.VMEM((tm, N_HEADS * V_HEAD_DIM), BF16)],
        compiler_params=_params("parallel", "parallel"),
        name="out_proj",
    )(o, o_last, x, gate, w_o)


ROW_TILE = 512
FFN_ROWS = 1024
Q_ROWS = 1024
FFN_DOWN_COLS = 512
FFN_TILE = 512
MOD_TILE = 256
CAST_ROWS = 256


def _swap_halves(w):
    half = w.shape[-1] // 2
    return jnp.concatenate([w[..., half:], w[..., :half]], axis=-1)


def _query_slots(w_uq):
    r = w_uq.shape[0]
    w = w_uq.reshape(r, N_HEADS, Q_HEAD_DIM)
    rope = w[..., QK_NOPE_DIM:]
    return jnp.concatenate([w[..., :QK_NOPE_DIM], rope, _swap_halves(rope)], axis=-1).reshape(
        r, N_HEADS * HEAD_SLOT)


def kernel(x, c, positions, mod_w, mod_b, norm_mix, norm_ffn, pool_w, pool_scale, kv_mod_w, kv_mod_b,
           kv_in_norm, w_dkv, kv_norm, w_uk, w_uv, w_kr, w_dq, q_norm, w_uq, w_o, ffn_gate, ffn_up,
           ffn_down, final_norm):
    b, s, d = x.shape

    c_rows = jnp.zeros((8, d), F32).at[:b].set(c)
    mod = _modulation(c_rows, mod_w, mod_b[:, None, :], MOD_TILE)[:, :b]
    kv_mod = _modulation(c_rows, kv_mod_w[None], kv_mod_b[None, None, :], MOD_TILE)[0, :b]
    table = _rope_table(positions, ROW_TILE)

    def vecs(a, n):
        return [v[:, None, :] for v in jnp.split(a, n, axis=-1)]

    wg = _cast_retile(ffn_gate, FFN_TILE, CAST_ROWS)
    wu = _cast_retile(ffn_up, FFN_TILE, CAST_ROWS)
    wd = _cast_retile(ffn_down, d, CAST_ROWS).reshape(ffn_down.shape)

    kv = None
    for i in range(DEPTH):
        shift_m, scale_m, gate_m, shift_f, scale_f, gate_f = vecs(mod[i], N_MOD)
        if i == N_A:
            kv_shift, kv_scale = vecs(kv_mod, 2)
            w_kr2 = jnp.concatenate([w_kr, _swap_halves(w_kr)], axis=-1).astype(BF16)
            kv = _shared_kv(x, kv_shift, kv_scale, kv_in_norm[None], w_dkv.astype(BF16), kv_norm[None],
                            w_uk.astype(BF16), w_uv.astype(BF16), w_kr2, table, ROW_TILE)
        if i < N_A:
            x = _pool_layer(x, shift_m, scale_m, gate_m, norm_mix[i][None], pool_w[i].astype(BF16),
                            pool_scale[i][None], ROW_TILE)
        else:
            j = i - N_A
            q = _queries(x, shift_m, scale_m, norm_mix[i][None], w_dq[j].astype(BF16), q_norm[j][None],
                         _query_slots(w_uq[j]).astype(BF16), table, Q_ROWS)
            o, o_last = _attention(q, kv[0], kv[1], ROW_TILE)
            x = _out_proj(o, o_last, x, gate_m, w_o[j].astype(BF16))
        x = _ffn_layer(x, shift_f, scale_f, gate_f, norm_ffn[i][None], wg, wu, wd, final_norm[None],
                       layer=i, tm=FFN_ROWS, final_norm=(i == DEPTH - 1))
    return x
```

```python
import functools
import math

import numpy as np
import jax
import jax.numpy as jnp
from jax import lax
from jax.experimental import pallas as pl
from jax.experimental.pallas import tpu as pltpu

D_MODEL = 2048
DEPTH = 4
N_A = DEPTH // 2
POOL_WINDOWS = (2, 4, 8, 16)
N_POOL_GROUPS = len(POOL_WINDOWS)
POOL_GROUP_DIM = D_MODEL // N_POOL_GROUPS
POOL_HALO = 16
QK_NOPE_DIM = 128
QK_ROPE_DIM = 64
V_HEAD_DIM = 128
N_HEADS = D_MODEL // V_HEAD_DIM
KV_LORA_RANK = D_MODEL // 4
Q_LORA_RANK = 512
Q_HEAD_DIM = QK_NOPE_DIM + QK_ROPE_DIM
HEAD_SLOT = 256
SM_SCALE = Q_HEAD_DIM ** -0.5
ROPE_THETA = 10000.0
FFN_DIM = 5632
N_MOD = 6
EPS = 1e-6
LOG2E = math.log2(math.e)
NEG_BIG = -1e30

LANES = 128
VMEM_LIMIT = 60000 * 1024

BF16 = jnp.bfloat16
F32 = jnp.float32


def _params(*semantics):
    return pltpu.CompilerParams(dimension_semantics=semantics, vmem_limit_bytes=VMEM_LIMIT)


def _dot(a, b):
    return jnp.dot(a, b, preferred_element_type=F32)


def _norm_mod(x, gain, shift):
    ms = jnp.mean(x * x, axis=-1, keepdims=True)
    return x * lax.rsqrt(ms + EPS) * gain + shift


def _rmsnorm(x, g):
    ms = jnp.mean(x * x, axis=-1, keepdims=True)
    return x * lax.rsqrt(ms + EPS) * g


NORM_ROWS = 16


def _mod_kernel(c_ref, w_ref, b_ref, o_ref):
    k = pl.program_id(1)

    @pl.when(k == 0)
    def _():
        o_ref[...] = jnp.broadcast_to(b_ref[...], o_ref.shape)

    c = c_ref[...]
    sc = (c * jax.nn.sigmoid(c)).astype(BF16)
    o_ref[...] += _dot(sc, w_ref[...].astype(BF16))


def _modulation(c_rows, w, b, tk):
    n_layers, d, n = w.shape
    rows = c_rows.shape[0]
    return pl.pallas_call(
        _mod_kernel,
        grid=(n_layers, d // tk),
        in_specs=[
            pl.BlockSpec((rows, tk), lambda l, k: (0, k)),
            pl.BlockSpec((None, tk, n), lambda l, k: (l, k, 0)),
            pl.BlockSpec((None, 1, n), lambda l, k: (l, 0, 0)),
        ],
        out_specs=pl.BlockSpec((None, rows, n), lambda l, k: (l, 0, 0)),
        out_shape=jax.ShapeDtypeStruct((n_layers, rows, n), F32),
        compiler_params=_params("parallel", "arbitrary"),
        name="modulation",
    )(c_rows, w, b)


def _rope_table_kernel(pos_ref, freq_ref, o_ref):
    ang = pos_ref[0].astype(F32) * freq_ref[...]
    lane = lax.broadcasted_iota(jnp.int32, ang.shape, 1)
    sin = jnp.sin(ang)
    o_ref[0] = jnp.where(lane < 64, jnp.cos(ang), jnp.where(lane < 96, -sin, sin))


def _rope_table(positions, ts):
    b, s = positions.shape
    half = QK_ROPE_DIM // 2
    inv_freq = 1.0 / (ROPE_THETA ** (np.arange(0, QK_ROPE_DIM, 2, dtype=np.float32) / QK_ROPE_DIM))
    freq = jnp.asarray(np.tile(inv_freq.astype(np.float32), LANES // half)[None, :])
    return pl.pallas_call(
        _rope_table_kernel,
        grid=(b, s // ts),
        in_specs=[
            pl.BlockSpec((1, ts, 1), lambda i, t: (i, t, 0)),
            pl.BlockSpec((1, LANES), lambda i, t: (0, 0)),
        ],
        out_specs=pl.BlockSpec((1, ts, LANES), lambda i, t: (i, t, 0)),
        out_shape=jax.ShapeDtypeStruct((b, s, LANES), F32),
        compiler_params=_params("parallel", "parallel"),
        name="rope_table",
    )(positions.reshape(b, s, 1), freq)


def _apply_rope(t2, table):
    prod = t2 * table
    return prod + pltpu.roll(prod, 64, axis=1)


def _pool_kernel(x_ref, halo_ref, shift_ref, scale_ref, gate_ref, g_ref, w_ref, ps_ref, o_ref):
    t = pl.program_id(1)
    ts = x_ref.shape[1]
    x = x_ref[0]
    gain = g_ref[...] * (1.0 + scale_ref[0])
    shift = shift_ref[0]
    h = _norm_mod(x, gain, shift)
    hh = _norm_mod(halo_ref[0], gain, shift) * (t > 0).astype(F32)
    hs = jnp.concatenate([hh, h], axis=0)
    seen = (t * ts + 1 + lax.broadcasted_iota(jnp.int32, (ts, LANES), 0)).astype(F32)
    out_scale = gate_ref[0] * ps_ref[...]
    for g, w in enumerate(POOL_WINDOWS):
        lo, hi = g * POOL_GROUP_DIM, (g + 1) * POOL_GROUP_DIM
        s = hs[:, lo:hi]
        k = 1
        while k < w:
            s = s + pltpu.roll(s, k, axis=0)
            k *= 2
        inv_cnt = 1.0 / jnp.minimum(seen, float(w))
        d = jnp.concatenate(
            [s[POOL_HALO:, c:c + LANES] * inv_cnt - h[:, lo + c:lo + c + LANES]
             for c in range(0, POOL_GROUP_DIM, LANES)], axis=1)
        y = _dot(d.astype(BF16), w_ref[g])
        o_ref[0, :, lo:hi] = x[:, lo:hi] + out_scale[:, lo:hi] * y


def _pool_layer(x, shift, scale, gate, norm_w, pool_w, pool_scale, ts):
    b, s, d = x.shape
    halo_blocks = ts // POOL_HALO
    vec = pl.BlockSpec((1, 1, d), lambda i, t: (i, 0, 0))
    row = pl.BlockSpec((1, d), lambda i, t: (0, 0))
    return pl.pallas_call(
        _pool_kernel,
        grid=(b, s // ts),
        in_specs=[
            pl.BlockSpec((1, ts, d), lambda i, t: (i, t, 0)),
            pl.BlockSpec((1, POOL_HALO, d), lambda i, t: (i, jnp.maximum(t * halo_blocks - 1, 0), 0)),
            vec, vec, vec, row,
            pl.BlockSpec((N_POOL_GROUPS, POOL_GROUP_DIM, POOL_GROUP_DIM), lambda i, t: (0, 0, 0)),
            row,
        ],
        out_specs=pl.BlockSpec((1, ts, d), lambda i, t: (i, t, 0)),
        out_shape=jax.ShapeDtypeStruct((b, s, d), F32),
        compiler_params=_params("parallel", "parallel"),
        name="pool_layer",
    )(x, x, shift, scale, gate, norm_w, pool_w, pool_scale)


def _ffn_kernel(x_hbm, shift_ref, scale_ref, gate_ref, g_ref, wg_ref, wu_ref, wd_ref, fn_ref,
                o_ref, xbuf, h_scr, sem, *, final_norm):
    n, f = pl.program_id(0), pl.program_id(1)
    tm, d = o_ref.shape[1:]
    row_chunks = [slice(r, r + SUB_ROWS) for r in range(0, tm, SUB_ROWS)]

    def fetch(tile):
        return pltpu.make_async_copy(x_hbm.at[tile], xbuf, sem)

    @pl.when(jnp.logical_and(n == 0, f == 0))
    def _():
        fetch(0).start()

    @pl.when(f == 0)
    def _():
        fetch(n).wait()
        gain = g_ref[...] * (1.0 + scale_ref[0])
        shift = shift_ref[0]
        for r in range(0, tm, NORM_ROWS):
            rows = slice(r, r + NORM_ROWS)
            x = xbuf[rows, :]
            h_scr[rows, :] = _norm_mod(x, gain, shift).astype(BF16)
            o_ref[0, rows, :] = x

    @pl.when(jnp.logical_and(f == 1, n + 1 < pl.num_programs(0)))
    def _():
        fetch(n + 1).start()

    h = h_scr[...]
    a = _dot(h, wg_ref[...])
    u = _dot(h, wu_ref[...])
    act = (a * jax.nn.sigmoid(a) * u).astype(BF16)
    gate = gate_ref[0]
    for c0 in range(0, d, FFN_DOWN_COLS):
        cols = slice(c0, c0 + FFN_DOWN_COLS)
        o_ref[0, :, cols] += gate[:, cols] * _dot(act, wd_ref[:, cols])

    if final_norm:
        @pl.when(f == pl.num_programs(1) - 1)
        def _():
            for rows in row_chunks:
                o_ref[0, rows, :] = _rmsnorm(o_ref[0, rows, :], fn_ref[...])


def _retile_kernel(w_ref, o_ref):
    tc = o_ref.shape[2]
    for j in range(o_ref.shape[0]):
        o_ref[j] = w_ref[:, j * tc:(j + 1) * tc].astype(o_ref.dtype)


def _cast_retile(w, tc, tr):
    n_layers, r, c = w.shape
    return pl.pallas_call(
        _retile_kernel,
        grid=(n_layers, r // tr),
        in_specs=[pl.BlockSpec((None, tr, c), lambda l, i: (l, i, 0))],
        out_specs=pl.BlockSpec((None, c // tc, tr, tc), lambda l, i: (l, 0, i, 0)),
        out_shape=jax.ShapeDtypeStruct((n_layers, c // tc, r, tc), BF16),
        compiler_params=_params("parallel", "parallel"),
        name="cast_retile",
    )(w)


def _ffn_layer(x, shift, scale, gate, norm_w, wg, wu, wd, final_w, layer, tm, final_norm):
    b, s, d = x.shape
    _, nf, _, tf = wg.shape
    nt = s // tm
    vec = pl.BlockSpec((1, 1, d), lambda n, f: (n // nt, 0, 0))
    row = pl.BlockSpec((1, d), lambda n, f: (0, 0))
    return pl.pallas_call(
        functools.partial(_ffn_kernel, final_norm=final_norm),
        grid=(b * nt, nf),
        in_specs=[
            pl.BlockSpec(memory_space=pl.ANY),
            vec, vec, vec, row,
            pl.BlockSpec((None, None, d, tf), lambda n, f: (layer, f, 0, 0)),
            pl.BlockSpec((None, None, d, tf), lambda n, f: (layer, f, 0, 0)),
            pl.BlockSpec((None, tf, d), lambda n, f: (layer, f, 0)),
            row,
        ],
        out_specs=pl.BlockSpec((1, tm, d), lambda n, f: (n // nt, n % nt, 0)),
        out_shape=jax.ShapeDtypeStruct((b, s, d), F32),
        scratch_shapes=[pltpu.VMEM((tm, d), F32), pltpu.VMEM((tm, d), BF16), pltpu.SemaphoreType.DMA(())],
        compiler_params=_params("arbitrary", "arbitrary"),
        name="ffn_layer",
    )(x.reshape(b * nt, tm, d), shift, scale, gate, norm_w, wg, wu, wd, final_w)


SUB_ROWS = 256


def _kv_kernel(x_ref, shift_ref, scale_ref, g_ref, wdkv_ref, kvn_ref, wuk_ref, wuv_ref, wkr_ref,
               tab_ref, k_ref, v_ref):
    gain = g_ref[...] * (1.0 + scale_ref[0])
    shift = shift_ref[0]
    for r0 in range(0, x_ref.shape[1], SUB_ROWS):
        rows = slice(r0, r0 + SUB_ROWS)
        h = _norm_mod(x_ref[0, rows, :], gain, shift).astype(BF16)
        ckv = _rmsnorm(_dot(h, wdkv_ref[...]), kvn_ref[...]).astype(BF16)
        kn = _dot(ckv, wuk_ref[...]).astype(BF16)
        v = _dot(ckv, wuv_ref[...]).astype(BF16)
        kr = _apply_rope(_dot(h, wkr_ref[...]), tab_ref[0, rows, :])
        lane = lax.broadcasted_iota(jnp.int32, kr.shape, 1)
        kr = jnp.where(lane < QK_ROPE_DIM, kr, 0.0).astype(BF16)
        for hd in range(N_HEADS):
            k_ref[0, hd, rows, :QK_NOPE_DIM] = kn[:, hd * QK_NOPE_DIM:(hd + 1) * QK_NOPE_DIM]
            k_ref[0, hd, rows, QK_NOPE_DIM:] = kr
            v_ref[0, hd, rows, :] = v[:, hd * V_HEAD_DIM:(hd + 1) * V_HEAD_DIM]


def _shared_kv(x, shift, scale, norm_w, w_dkv, kv_norm, w_uk, w_uv, w_kr2, table, ts):
    b, s, d = x.shape
    full = lambda a: pl.BlockSpec(a.shape, lambda i, t: (0,) * a.ndim)
    vec = pl.BlockSpec((1, 1, d), lambda i, t: (i, 0, 0))
    return pl.pallas_call(
        _kv_kernel,
        grid=(b, s // ts),
        in_specs=[
            pl.BlockSpec((1, ts, d), lambda i, t: (i, t, 0)),
            vec, vec, full(norm_w), full(w_dkv), full(kv_norm), full(w_uk), full(w_uv), full(w_kr2),
            pl.BlockSpec((1, ts, LANES), lambda i, t: (i, t, 0)),
        ],
        out_specs=[
            pl.BlockSpec((1, N_HEADS, ts, HEAD_SLOT), lambda i, t: (i, 0, t, 0)),
            pl.BlockSpec((1, N_HEADS, ts, V_HEAD_DIM), lambda i, t: (i, 0, t, 0)),
        ],
        out_shape=[
            jax.ShapeDtypeStruct((b, N_HEADS, s, HEAD_SLOT), BF16),
            jax.ShapeDtypeStruct((b, N_HEADS, s, V_HEAD_DIM), BF16),
        ],
        compiler_params=_params("parallel", "parallel"),
        name="shared_kv",
    )(x, shift, scale, norm_w, w_dkv, kv_norm, w_uk, w_uv, w_kr2, table)


def _q_kernel(x_ref, shift_ref, scale_ref, g_ref, wdq_ref, qn_ref, wuq_ref, tab_ref, q_ref):
    gain = g_ref[...] * (1.0 + scale_ref[0])
    h = _norm_mod(x_ref[0], gain, shift_ref[0]).astype(BF16)
    cq = _rmsnorm(_dot(h, wdq_ref[...]), qn_ref[...]).astype(BF16)
    table = tab_ref[0] * (SM_SCALE * LOG2E)
    for hd in range(N_HEADS):
        q = _dot(cq, wuq_ref[:, hd * HEAD_SLOT:(hd + 1) * HEAD_SLOT])
        q_ref[0, hd, :, :QK_NOPE_DIM] = (q[:, :QK_NOPE_DIM] * (SM_SCALE * LOG2E)).astype(BF16)
        q_ref[0, hd, :, QK_NOPE_DIM:] = _apply_rope(q[:, QK_NOPE_DIM:], table).astype(BF16)


def _queries(x, shift, scale, norm_w, w_dq, q_norm, w_uq_slots, table, ts):
    b, s, d = x.shape
    full = lambda a: pl.BlockSpec(a.shape, lambda i, t: (0,) * a.ndim)
    vec = pl.BlockSpec((1, 1, d), lambda i, t: (i, 0, 0))
    return pl.pallas_call(
        _q_kernel,
        grid=(b, s // ts),
        in_specs=[
            pl.BlockSpec((1, ts, d), lambda i, t: (i, t, 0)),
            vec, vec, full(norm_w), full(w_dq), full(q_norm), full(w_uq_slots),
            pl.BlockSpec((1, ts, LANES), lambda i, t: (i, t, 0)),
        ],
        out_specs=pl.BlockSpec((1, N_HEADS, ts, HEAD_SLOT), lambda i, t: (i, 0, t, 0)),
        out_shape=jax.ShapeDtypeStruct((b, N_HEADS, s, HEAD_SLOT), BF16),
        compiler_params=_params("parallel", "parallel"),
        name="queries",
    )(x, shift, scale, norm_w, w_dq, q_norm, w_uq_slots, table)


def _lane_fold(x, op):
    r = x[:, :LANES]
    for c in range(1, x.shape[1] // LANES):
        r = op(r, x[:, c * LANES:(c + 1) * LANES])
    return r


def _attn_kernel(q_ref, k_ref, v_ref, vprev_ref, o_ref, olast_ref, s_scr, slast_scr, mlast_scr, *, tq):
    seq = q_ref.shape[2]
    nq = seq // tq
    lane = lax.broadcasted_iota(jnp.int32, (tq, LANES), 1)
    ones_col = jnp.where(lane == 0, 1.0, 0.0).astype(BF16)

    half = tq // 2

    def probs(s, mb):
        ps = [jnp.exp2(s[:, c * LANES:(c + 1) * LANES] - mb) for c in range(s.shape[1] // LANES)]
        return jnp.concatenate(ps, axis=1).astype(BF16)

    def pv_pass(scores, mb, values, n_kv):
        acc = None
        for j in range(n_kv):
            vv = jnp.concatenate([values[0, 0, j * tq:(j + 1) * tq, :], ones_col], axis=1)
            if j < n_kv - 1:
                d = _dot(probs(scores[j], mb), vv)
            else:
                d = _dot(probs(scores[j, :, :half], mb), vv[:half])
                d_low = _dot(probs(scores[j, half:, half:], mb[half:]), vv[half:])
                d = jnp.concatenate([d[:half], d[half:] + d_low], axis=0)
            acc = d if acc is None else acc + d
        return (acc[:, :V_HEAD_DIM] / acc[:, V_HEAD_DIM:V_HEAD_DIM + 1]).astype(o_ref.dtype)

    def causal(s, row0):
        r = lax.broadcasted_iota(jnp.int32, s.shape, 0) + row0
        c = lax.broadcasted_iota(jnp.int32, s.shape, 1)
        return jnp.where(c <= r, s, NEG_BIG)

    def qk(q, k):
        return lax.dot_general(q, k, (((1,), (1,)), ((), ())), preferred_element_type=F32)

    @pl.when(pl.program_id(0) == 0)
    def _():
        slast_scr[...] = jnp.zeros(slast_scr.shape, F32)
        mlast_scr[...] = jnp.zeros(mlast_scr.shape, F32)

    olast_ref[0, 0] = pv_pass(slast_scr, mlast_scr[...], vprev_ref, nq)

    for t in range(nq):
        rows = slice(t * tq, (t + 1) * tq)
        q = q_ref[0, 0, rows, :]
        scores = s_scr if t < nq - 1 else slast_scr
        m_run = None
        for j in range(t):
            s = qk(q, k_ref[0, 0, j * tq:(j + 1) * tq, :])
            scores[j] = s
            f = _lane_fold(s, jnp.maximum)
            m_run = f if m_run is None else jnp.maximum(m_run, f)
        s_left = causal(qk(q, k_ref[0, 0, t * tq:t * tq + half, :]), 0)
        s_low = causal(qk(q[half:], k_ref[0, 0, t * tq + half:(t + 1) * tq, :]), 0)
        scores[t, :, :half] = s_left
        scores[t, half:, half:] = s_low
        f = _lane_fold(s_left, jnp.maximum)
        f = jnp.concatenate([f[:half], jnp.maximum(f[half:], _lane_fold(s_low, jnp.maximum))], axis=0)
        m_run = f if m_run is None else jnp.maximum(m_run, f)
        mb = jnp.broadcast_to(jnp.max(m_run, axis=-1, keepdims=True), (tq, LANES))
        if t < nq - 1:
            o_ref[0, 0, rows, :] = pv_pass(s_scr, mb, v_ref, t + 1)
        else:
            mlast_scr[...] = mb
            o_ref[0, 0, rows, :] = jnp.zeros((tq, V_HEAD_DIM), o_ref.dtype)


def _attention(q, k, v, tq):
    b, nh, s, _ = q.shape
    n_heads_total = b * nh

    def cur(n):
        m = jnp.minimum(n, n_heads_total - 1)
        return (m // nh, m % nh, 0, 0)

    def prev(n):
        m = jnp.maximum(n - 1, 0)
        return (m // nh, m % nh, 0, 0)

    return pl.pallas_call(
        functools.partial(_attn_kernel, tq=tq),
        grid=(n_heads_total + 1,),
        in_specs=[
            pl.BlockSpec((1, 1, s, HEAD_SLOT), cur),
            pl.BlockSpec((1, 1, s, HEAD_SLOT), cur),
            pl.BlockSpec((1, 1, s, V_HEAD_DIM), cur),
            pl.BlockSpec((1, 1, s, V_HEAD_DIM), prev),
        ],
        out_specs=[
            pl.BlockSpec((1, 1, s, V_HEAD_DIM), cur),
            pl.BlockSpec((1, 1, tq, V_HEAD_DIM), prev),
        ],
        out_shape=[
            jax.ShapeDtypeStruct((b, nh, s, V_HEAD_DIM), BF16),
            jax.ShapeDtypeStruct((b, nh, tq, V_HEAD_DIM), BF16),
        ],
        scratch_shapes=[
            pltpu.VMEM((s // tq - 1, tq, tq), F32),
            pltpu.VMEM((s // tq, tq, tq), F32),
            pltpu.VMEM((tq, LANES), F32),
        ],
        compiler_params=_params("arbitrary"),
        name="attention",
    )(q, k, v, v)


def _oproj_kernel(o_ref, olast_ref, x_ref, gate_ref, wo_ref, out_ref, cat_scr):
    last = pl.program_id(1) == pl.num_programs(1) - 1
    for hd in range(N_HEADS):
        cat_scr[:, hd * V_HEAD_DIM:(hd + 1) * V_HEAD_DIM] = jnp.where(last, olast_ref[0, hd], o_ref[0, hd])
    out_ref[0] = x_ref[0] + gate_ref[0] * _dot(cat_scr[...], wo_ref[...])


def _out_proj(o, o_last, x, gate, w_o):
    b, s, d = x.shape
    tm = o_last.shape[2]
    return pl.pallas_call(
        _oproj_kernel,
        grid=(b, s // tm),
        in_specs=[
            pl.BlockSpec((1, N_HEADS, tm, V_HEAD_DIM), lambda i, t: (i, 0, t, 0)),
            pl.BlockSpec((1, N_HEADS, tm, V_HEAD_DIM), lambda i, t: (i, 0, 0, 0)),
            pl.BlockSpec((1, tm, d), lambda i, t: (i, t, 0)),
            pl.BlockSpec((1, 1, d), lambda i, t: (i, 0, 0)),
            pl.BlockSpec(w_o.shape, lambda i, t: (0, 0)),
        ],
        out_specs=pl.BlockSpec((1, tm, d), lambda i, t: (i, t, 0)),
        out_shape=jax.ShapeDtypeStruct((b, s, d), F32),
        scratch_shapes=[pltpu.VMEM((tm, N_HEADS * V_HEAD_DIM), BF16)],
        compiler_params=_params("parallel", "parallel"),
        name="out_proj",
    )(o, o_last, x, gate, w_o)


ROW_TILE = 512
FFN_ROWS = 1024
Q_ROWS = 1024
FFN_DOWN_COLS = 512
FFN_TILE = 512
MOD_TILE = 256
CAST_ROWS = 256


def _swap_halves(w):
    half = w.shape[-1] // 2
    return jnp.concatenate([w[..., half:], w[..., :half]], axis=-1)


def _query_slots(w_uq):
    r = w_uq.shape[0]
    w = w_uq.reshape(r, N_HEADS, Q_HEAD_DIM)
    rope = w[..., QK_NOPE_DIM:]
    return jnp.concatenate([w[..., :QK_NOPE_DIM], rope, _swap_halves(rope)], axis=-1).reshape(
        r, N_HEADS * HEAD_SLOT)


def kernel(x, c, positions, mod_w, mod_b, norm_mix, norm_ffn, pool_w, pool_scale, kv_mod_w, kv_mod_b,
           kv_in_norm, w_dkv, kv_norm, w_uk, w_uv, w_kr, w_dq, q_norm, w_uq, w_o, ffn_gate, ffn_up,
           ffn_down, final_norm):
    b, s, d = x.shape

    c_rows = jnp.zeros((8, d), F32).at[:b].set(c)
    mod = _modulation(c_rows, mod_w, mod_b[:, None, :], MOD_TILE)[:, :b]
    kv_mod = _modulation(c_rows, kv_mod_w[None], kv_mod_b[None, None, :], MOD_TILE)[0, :b]
    table = _rope_table(positions, ROW_TILE)

    def vecs(a, n):
        return [v[:, None, :] for v in jnp.split(a, n, axis=-1)]

    wg = _cast_retile(ffn_gate, FFN_TILE, CAST_ROWS)
    wu = _cast_retile(ffn_up, FFN_TILE, CAST_ROWS)
    wd = _cast_retile(ffn_down, d, FFN_TILE).reshape(ffn_down.shape)

    kv = None
    for i in range(DEPTH):
        shift_m, scale_m, gate_m, shift_f, scale_f, gate_f = vecs(mod[i], N_MOD)
        if i == N_A:
            kv_shift, kv_scale = vecs(kv_mod, 2)
            w_kr2 = jnp.concatenate([w_kr, _swap_halves(w_kr)], axis=-1).astype(BF16)
            kv = _shared_kv(x, kv_shift, kv_scale, kv_in_norm[None], w_dkv.astype(BF16), kv_norm[None],
                            w_uk.astype(BF16), w_uv.astype(BF16), w_kr2, table, ROW_TILE)
        if i < N_A:
            x = _pool_layer(x, shift_m, scale_m, gate_m, norm_mix[i][None], pool_w[i].astype(BF16),
                            pool_scale[i][None], ROW_TILE)
        else:
            j = i - N_A
            q = _queries(x, shift_m, scale_m, norm_mix[i][None], w_dq[j].astype(BF16), q_norm[j][None],
                         _query_slots(w_uq[j]).astype(BF16), table, Q_ROWS)
            o, o_last = _attention(q, kv[0], kv[1], ROW_TILE)
            x = _out_proj(o, o_last, x, gate_m, w_o[j].astype(BF16))
        x = _ffn_layer(x, shift_f, scale_f, gate_f, norm_ffn[i][None], wg, wu, wd, final_norm[None],
                       layer=i, tm=FFN_ROWS, final_norm=(i == DEPTH - 1))
    return x
```

```python
import functools
import math

import numpy as np
import jax
import jax.numpy as jnp
from jax import lax
from jax.experimental import pallas as pl
from jax.experimental.pallas import tpu as pltpu

D_MODEL = 2048
DEPTH = 4
N_A = DEPTH // 2
POOL_WINDOWS = (2, 4, 8, 16)
N_POOL_GROUPS = len(POOL_WINDOWS)
POOL_GROUP_DIM = D_MODEL // N_POOL_GROUPS
POOL_HALO = 16
QK_NOPE_DIM = 128
QK_ROPE_DIM = 64
V_HEAD_DIM = 128
N_HEADS = D_MODEL // V_HEAD_DIM
KV_LORA_RANK = D_MODEL // 4
Q_LORA_RANK = 512
Q_HEAD_DIM = QK_NOPE_DIM + QK_ROPE_DIM
HEAD_SLOT = 256
SM_SCALE = Q_HEAD_DIM ** -0.5
ROPE_THETA = 10000.0
FFN_DIM = 5632
N_MOD = 6
EPS = 1e-6
LOG2E = math.log2(math.e)
NEG_BIG = -1e30

LANES = 128
VMEM_LIMIT = 60000 * 1024

BF16 = jnp.bfloat16
F32 = jnp.float32


def _params(*semantics):
    return pltpu.CompilerParams(dimension_semantics=semantics, vmem_limit_bytes=VMEM_LIMIT)


def _dot(a, b):
    return jnp.dot(a, b, preferred_element_type=F32)


def _norm_mod(x, gain, shift):
    ms = jnp.mean(x * x, axis=-1, keepdims=True)
    return x * lax.rsqrt(ms + EPS) * gain + shift


def _rmsnorm(x, g):
    ms = jnp.mean(x * x, axis=-1, keepdims=True)
    return x * lax.rsqrt(ms + EPS) * g


NORM_ROWS = 16


def _mod_kernel(c_ref, w_ref, b_ref, o_ref):
    k = pl.program_id(1)

    @pl.when(k == 0)
    def _():
        o_ref[...] = jnp.broadcast_to(b_ref[...], o_ref.shape)

    c = c_ref[...]
    sc = (c * jax.nn.sigmoid(c)).astype(BF16)
    o_ref[...] += _dot(sc, w_ref[...].astype(BF16))


def _modulation(c_rows, w, b, tk):
    n_layers, d, n = w.shape
    rows = c_rows.shape[0]
    return pl.pallas_call(
        _mod_kernel,
        grid=(n_layers, d // tk),
        in_specs=[
            pl.BlockSpec((rows, tk), lambda l, k: (0, k)),
            pl.BlockSpec((None, tk, n), lambda l, k: (l, k, 0)),
            pl.BlockSpec((None, 1, n), lambda l, k: (l, 0, 0)),
        ],
        out_specs=pl.BlockSpec((None, rows, n), lambda l, k: (l, 0, 0)),
        out_shape=jax.ShapeDtypeStruct((n_layers, rows, n), F32),
        compiler_params=_params("parallel", "arbitrary"),
        name="modulation",
    )(c_rows, w, b)


def _rope_table_kernel(pos_ref, freq_ref, o_ref):
    ang = pos_ref[0].astype(F32) * freq_ref[...]
    lane = lax.broadcasted_iota(jnp.int32, ang.shape, 1)
    sin = jnp.sin(ang)
    o_ref[0] = jnp.where(lane < 64, jnp.cos(ang), jnp.where(lane < 96, -sin, sin))


def _rope_table(positions, ts):
    b, s = positions.shape
    half = QK_ROPE_DIM // 2
    inv_freq = 1.0 / (ROPE_THETA ** (np.arange(0, QK_ROPE_DIM, 2, dtype=np.float32) / QK_ROPE_DIM))
    freq = jnp.asarray(np.tile(inv_freq.astype(np.float32), LANES // half)[None, :])
    return pl.pallas_call(
        _rope_table_kernel,
        grid=(b, s // ts),
        in_specs=[
            pl.BlockSpec((1, ts, 1), lambda i, t: (i, t, 0)),
            pl.BlockSpec((1, LANES), lambda i, t: (0, 0)),
        ],
        out_specs=pl.BlockSpec((1, ts, LANES), lambda i, t: (i, t, 0)),
        out_shape=jax.ShapeDtypeStruct((b, s, LANES), F32),
        compiler_params=_params("parallel", "parallel"),
        name="rope_table",
    )(positions.reshape(b, s, 1), freq)


def _apply_rope(t2, table):
    prod = t2 * table
    return prod + pltpu.roll(prod, 64, axis=1)


def _pool_kernel(x_ref, halo_ref, shift_ref, scale_ref, gate_ref, g_ref, w_ref, ps_ref, o_ref):
    t = pl.program_id(1)
    ts = x_ref.shape[1]
    x = x_ref[0]
    gain = g_ref[...] * (1.0 + scale_ref[0])
    shift = shift_ref[0]
    h = _norm_mod(x, gain, shift)
    hh = _norm_mod(halo_ref[0], gain, shift) * (t > 0).astype(F32)
    hs = jnp.concatenate([hh, h], axis=0)
    seen = (t * ts + 1 + lax.broadcasted_iota(jnp.int32, (ts, LANES), 0)).astype(F32)
    out_scale = gate_ref[0] * ps_ref[...]
    for g, w in enumerate(POOL_WINDOWS):
        lo, hi = g * POOL_GROUP_DIM, (g + 1) * POOL_GROUP_DIM
        s = hs[:, lo:hi]
        k = 1
        while k < w:
            s = s + pltpu.roll(s, k, axis=0)
            k *= 2
        inv_cnt = 1.0 / jnp.minimum(seen, float(w))
        d = jnp.concatenate(
            [s[POOL_HALO:, c:c + LANES] * inv_cnt - h[:, lo + c:lo + c + LANES]
             for c in range(0, POOL_GROUP_DIM, LANES)], axis=1)
        y = _dot(d.astype(BF16), w_ref[g])
        o_ref[0, :, lo:hi] = x[:, lo:hi] + out_scale[:, lo:hi] * y


def _pool_layer(x, shift, scale, gate, norm_w, pool_w, pool_scale, ts):
    b, s, d = x.shape
    halo_blocks = ts // POOL_HALO
    vec = pl.BlockSpec((1, 1, d), lambda i, t: (i, 0, 0))
    row = pl.BlockSpec((1, d), lambda i, t: (0, 0))
    return pl.pallas_call(
        _pool_kernel,
        grid=(b, s // ts),
        in_specs=[
            pl.BlockSpec((1, ts, d), lambda i, t: (i, t, 0)),
            pl.BlockSpec((1, POOL_HALO, d), lambda i, t: (i, jnp.maximum(t * halo_blocks - 1, 0), 0)),
            vec, vec, vec, row,
            pl.BlockSpec((N_POOL_GROUPS, POOL_GROUP_DIM, POOL_GROUP_DIM), lambda i, t: (0, 0, 0)),
            row,
        ],
        out_specs=pl.BlockSpec((1, ts, d), lambda i, t: (i, t, 0)),
        out_shape=jax.ShapeDtypeStruct((b, s, d), F32),
        compiler_params=_params("parallel", "parallel"),
        name="pool_layer",
    )(x, x, shift, scale, gate, norm_w, pool_w, pool_scale)


def _ffn_kernel(x_hbm, shift_ref, scale_ref, gate_ref, g_ref, wg_ref, wu_ref, wd_ref, fn_ref,
                o_ref, xbuf, h_scr, sem, *, final_norm):
    n, f = pl.program_id(0), pl.program_id(1)
    tm, d = o_ref.shape[1:]
    row_chunks = [slice(r, r + SUB_ROWS) for r in range(0, tm, SUB_ROWS)]

    def fetch(tile):
        return pltpu.make_async_copy(x_hbm.at[tile], xbuf, sem)

    @pl.when(jnp.logical_and(n == 0, f == 0))
    def _():
        fetch(0).start()

    @pl.when(f == 0)
    def _():
        fetch(n).wait()
        gain = g_ref[...] * (1.0 + scale_ref[0])
        shift = shift_ref[0]
        for r in range(0, tm, NORM_ROWS):
            rows = slice(r, r + NORM_ROWS)
            x = xbuf[rows, :]
            h_scr[rows, :] = _norm_mod(x, gain, shift).astype(BF16)
            o_ref[0, rows, :] = x

    @pl.when(jnp.logical_and(f == 1, n + 1 < pl.num_programs(0)))
    def _():
        fetch(n + 1).start()

    h = h_scr[...]
    tf = wg_ref.shape[1]
    acts = []
    for c0 in range(0, tf, tf // 2):
        a = _dot(h, wg_ref[:, c0:c0 + tf // 2])
        u = _dot(h, wu_ref[:, c0:c0 + tf // 2])
        acts.append((a * jax.nn.sigmoid(a) * u).astype(BF16))
    gate = gate_ref[0]
    for c0 in range(0, d, FFN_DOWN_COLS):
        cols = slice(c0, c0 + FFN_DOWN_COLS)
        part = _dot(acts[0], wd_ref[:tf // 2, cols]) + _dot(acts[1], wd_ref[tf // 2:, cols])
        o_ref[0, :, cols] += gate[:, cols] * part

    if final_norm:
        @pl.when(f == pl.num_programs(1) - 1)
        def _():
            for rows in row_chunks:
                o_ref[0, rows, :] = _rmsnorm(o_ref[0, rows, :], fn_ref[...])


def _retile_kernel(w_ref, o_ref):
    tc = o_ref.shape[2]
    for j in range(o_ref.shape[0]):
        o_ref[j] = w_ref[:, j * tc:(j + 1) * tc].astype(o_ref.dtype)


def _cast_retile(w, tc, tr):
    n_layers, r, c = w.shape
    return pl.pallas_call(
        _retile_kernel,
        grid=(n_layers, r // tr),
        in_specs=[pl.BlockSpec((None, tr, c), lambda l, i: (l, i, 0))],
        out_specs=pl.BlockSpec((None, c // tc, tr, tc), lambda l, i: (l, 0, i, 0)),
        out_shape=jax.ShapeDtypeStruct((n_layers, c // tc, r, tc), BF16),
        compiler_params=_params("parallel", "parallel"),
        name="cast_retile",
    )(w)


def _ffn_layer(x, shift, scale, gate, norm_w, wg, wu, wd, final_w, layer, tm, final_norm):
    b, s, d = x.shape
    _, nf, _, tf = wg.shape
    nt = s // tm
    vec = pl.BlockSpec((1, 1, d), lambda n, f: (n // nt, 0, 0))
    row = pl.BlockSpec((1, d), lambda n, f: (0, 0))
    return pl.pallas_call(
        functools.partial(_ffn_kernel, final_norm=final_norm),
        grid=(b * nt, nf),
        in_specs=[
            pl.BlockSpec(memory_space=pl.ANY),
            vec, vec, vec, row,
            pl.BlockSpec((None, None, d, tf), lambda n, f: (layer, f, 0, 0)),
            pl.BlockSpec((None, None, d, tf), lambda n, f: (layer, f, 0, 0)),
            pl.BlockSpec((None, tf, d), lambda n, f: (layer, f, 0)),
            row,
        ],
        out_specs=pl.BlockSpec((1, tm, d), lambda n, f: (n // nt, n % nt, 0)),
        out_shape=jax.ShapeDtypeStruct((b, s, d), F32),
        scratch_shapes=[pltpu.VMEM((tm, d), F32), pltpu.VMEM((tm, d), BF16), pltpu.SemaphoreType.DMA(())],
        compiler_params=_params("arbitrary", "arbitrary"),
        name="ffn_layer",
    )(x.reshape(b * nt, tm, d), shift, scale, gate, norm_w, wg, wu, wd, final_w)


SUB_ROWS = 256


def _kv_kernel(x_ref, shift_ref, scale_ref, g_ref, wdkv_ref, kvn_ref, wuk_ref, wuv_ref, wkr_ref,
               tab_ref, k_ref, v_ref):
    gain = g_ref[...] * (1.0 + scale_ref[0])
    shift = shift_ref[0]
    chains = [slice(r0, r0 + SUB_ROWS) for r0 in range(0, x_ref.shape[1], SUB_ROWS)]
    hs = [_norm_mod(x_ref[0, rows, :], gain, shift).astype(BF16) for rows in chains]
    cs = [_dot(h, wdkv_ref[...]) for h in hs]
    krs = [_dot(h, wkr_ref[...]) for h in hs]
    ckvs = [_rmsnorm(c, kvn_ref[...]).astype(BF16) for c in cs]
    for rows, ckv, kr in zip(chains, ckvs, krs):
        kn = _dot(ckv, wuk_ref[...]).astype(BF16)
        v = _dot(ckv, wuv_ref[...]).astype(BF16)
        kr = _apply_rope(kr, tab_ref[0, rows, :])
        lane = lax.broadcasted_iota(jnp.int32, kr.shape, 1)
        kr = jnp.where(lane < QK_ROPE_DIM, kr, 0.0).astype(BF16)
        for hd in range(N_HEADS):
            k_ref[0, hd, rows, :QK_NOPE_DIM] = kn[:, hd * QK_NOPE_DIM:(hd + 1) * QK_NOPE_DIM]
            k_ref[0, hd, rows, QK_NOPE_DIM:] = kr
            v_ref[0, hd, rows, :] = v[:, hd * V_HEAD_DIM:(hd + 1) * V_HEAD_DIM]


def _shared_kv(x, shift, scale, norm_w, w_dkv, kv_norm, w_uk, w_uv, w_kr2, table, ts):
    b, s, d = x.shape
    full = lambda a: pl.BlockSpec(a.shape, lambda i, t: (0,) * a.ndim)
    vec = pl.BlockSpec((1, 1, d), lambda i, t: (i, 0, 0))
    return pl.pallas_call(
        _kv_kernel,
        grid=(b, s // ts),
        in_specs=[
            pl.BlockSpec((1, ts, d), lambda i, t: (i, t, 0)),
            vec, vec, full(norm_w), full(w_dkv), full(kv_norm), full(w_uk), full(w_uv), full(w_kr2),
            pl.BlockSpec((1, ts, LANES), lambda i, t: (i, t, 0)),
        ],
        out_specs=[
            pl.BlockSpec((1, N_HEADS, ts, HEAD_SLOT), lambda i, t: (i, 0, t, 0)),
            pl.BlockSpec((1, N_HEADS, ts, V_HEAD_DIM), lambda i, t: (i, 0, t, 0)),
        ],
        out_shape=[
            jax.ShapeDtypeStruct((b, N_HEADS, s, HEAD_SLOT), BF16),
            jax.ShapeDtypeStruct((b, N_HEADS, s, V_HEAD_DIM), BF16),
        ],
        compiler_params=_params("parallel", "parallel"),
        name="shared_kv",
    )(x, shift, scale, norm_w, w_dkv, kv_norm, w_uk, w_uv, w_kr2, table)


def _q_kernel(x_ref, shift_ref, scale_ref, g_ref, wdq_ref, qn_ref, wuq_ref, tab_ref, q_ref):
    gain = g_ref[...] * (1.0 + scale_ref[0])
    h = _norm_mod(x_ref[0], gain, shift_ref[0]).astype(BF16)
    cq = _rmsnorm(_dot(h, wdq_ref[...]), qn_ref[...]).astype(BF16)
    table = tab_ref[0] * (SM_SCALE * LOG2E)
    for hd in range(N_HEADS):
        q = _dot(cq, wuq_ref[:, hd * HEAD_SLOT:(hd + 1) * HEAD_SLOT])
        q_ref[0, hd, :, :QK_NOPE_DIM] = (q[:, :QK_NOPE_DIM] * (SM_SCALE * LOG2E)).astype(BF16)
        q_ref[0, hd, :, QK_NOPE_DIM:] = _apply_rope(q[:, QK_NOPE_DIM:], table).astype(BF16)


def _queries(x, shift, scale, norm_w, w_dq, q_norm, w_uq_slots, table, ts):
    b, s, d = x.shape
    full = lambda a: pl.BlockSpec(a.shape, lambda i, t: (0,) * a.ndim)
    vec = pl.BlockSpec((1, 1, d), lambda i, t: (i, 0, 0))
    return pl.pallas_call(
        _q_kernel,
        grid=(b, s // ts),
        in_specs=[
            pl.BlockSpec((1, ts, d), lambda i, t: (i, t, 0)),
            vec, vec, full(norm_w), full(w_dq), full(q_norm), full(w_uq_slots),
            pl.BlockSpec((1, ts, LANES), lambda i, t: (i, t, 0)),
        ],
        out_specs=pl.BlockSpec((1, N_HEADS, ts, HEAD_SLOT), lambda i, t: (i, 0, t, 0)),
        out_shape=jax.ShapeDtypeStruct((b, N_HEADS, s, HEAD_SLOT), BF16),
        compiler_params=_params("parallel", "parallel"),
        name="queries",
    )(x, shift, scale, norm_w, w_dq, q_norm, w_uq_slots, table)


def _lane_fold(x, op):
    r = x[:, :LANES]
    for c in range(1, x.shape[1] // LANES):
        r = op(r, x[:, c * LANES:(c + 1) * LANES])
    return r


def _attn_kernel(q_ref, k_ref, v_ref, vprev_ref, o_ref, olast_ref, s_scr, slast_scr, mlast_scr, *, tq):
    seq = q_ref.shape[2]
    nq = seq // tq
    lane = lax.broadcasted_iota(jnp.int32, (tq, LANES), 1)
    ones_col = jnp.where(lane == 0, 1.0, 0.0).astype(BF16)

    half = tq // 2

    def probs(s, mb):
        ps = [jnp.exp2(s[:, c * LANES:(c + 1) * LANES] - mb) for c in range(s.shape[1] // LANES)]
        return jnp.concatenate(ps, axis=1).astype(BF16)

    def pv_pass(scores, mb, values, n_kv):
        acc = None
        for j in range(n_kv):
            vv = jnp.concatenate([values[0, 0, j * tq:(j + 1) * tq, :], ones_col], axis=1)
            if j < n_kv - 1:
                d = _dot(probs(scores[j], mb), vv)
            else:
                d = _dot(probs(scores[j, :, :half], mb), vv[:half])
                d_low = _dot(probs(scores[j, half:, half:], mb[half:]), vv[half:])
                d = jnp.concatenate([d[:half], d[half:] + d_low], axis=0)
            acc = d if acc is None else acc + d
        return (acc[:, :V_HEAD_DIM] / acc[:, V_HEAD_DIM:V_HEAD_DIM + 1]).astype(o_ref.dtype)

    def causal(s, row0):
        r = lax.broadcasted_iota(jnp.int32, s.shape, 0) + row0
        c = lax.broadcasted_iota(jnp.int32, s.shape, 1)
        return jnp.where(c <= r, s, NEG_BIG)

    def qk(q, k):
        return lax.dot_general(q, k, (((1,), (1,)), ((), ())), preferred_element_type=F32)

    @pl.when(pl.program_id(0) == 0)
    def _():
        slast_scr[...] = jnp.zeros(slast_scr.shape, F32)
        mlast_scr[...] = jnp.zeros(mlast_scr.shape, F32)

    olast_ref[0, 0] = pv_pass(slast_scr, mlast_scr[...], vprev_ref, nq)

    for t in range(nq):
        rows = slice(t * tq, (t + 1) * tq)
        q = q_ref[0, 0, rows, :]
        scores = s_scr if t < nq - 1 else slast_scr
        m_run = None
        for j in range(t):
            s = qk(q, k_ref[0, 0, j * tq:(j + 1) * tq, :])
            scores[j] = s
            f = _lane_fold(s, jnp.maximum)
            m_run = f if m_run is None else jnp.maximum(m_run, f)
        s_left = causal(qk(q, k_ref[0, 0, t * tq:t * tq + half, :]), 0)
        s_low = causal(qk(q[half:], k_ref[0, 0, t * tq + half:(t + 1) * tq, :]), 0)
        scores[t, :, :half] = s_left
        scores[t, half:, half:] = s_low
        f = _lane_fold(s_left, jnp.maximum)
        f = jnp.concatenate([f[:half], jnp.maximum(f[half:], _lane_fold(s_low, jnp.maximum))], axis=0)
        m_run = f if m_run is None else jnp.maximum(m_run, f)
        mb = jnp.broadcast_to(jnp.max(m_run, axis=-1, keepdims=True), (tq, LANES))
        if t < nq - 1:
            o_ref[0, 0, rows, :] = pv_pass(s_scr, mb, v_ref, t + 1)
        else:
            mlast_scr[...] = mb
            o_ref[0, 0, rows, :] = jnp.zeros((tq, V_HEAD_DIM), o_ref.dtype)


def _attention(q, k, v, tq):
    b, nh, s, _ = q.shape
    n_heads_total = b * nh

    def cur(n):
        m = jnp.minimum(n, n_heads_total - 1)
        return (m // nh, m % nh, 0, 0)

    def prev(n):
        m = jnp.maximum(n - 1, 0)
        return (m // nh, m % nh, 0, 0)

    return pl.pallas_call(
        functools.partial(_attn_kernel, tq=tq),
        grid=(n_heads_total + 1,),
        in_specs=[
            pl.BlockSpec((1, 1, s, HEAD_SLOT), cur),
            pl.BlockSpec((1, 1, s, HEAD_SLOT), cur),
            pl.BlockSpec((1, 1, s, V_HEAD_DIM), cur),
            pl.BlockSpec((1, 1, s, V_HEAD_DIM), prev),
        ],
        out_specs=[
            pl.BlockSpec((1, 1, s, V_HEAD_DIM), cur),
            pl.BlockSpec((1, 1, tq, V_HEAD_DIM), prev),
        ],
        out_shape=[
            jax.ShapeDtypeStruct((b, nh, s, V_HEAD_DIM), BF16),
            jax.ShapeDtypeStruct((b, nh, tq, V_HEAD_DIM), BF16),
        ],
        scratch_shapes=[
            pltpu.VMEM((s // tq - 1, tq, tq), F32),
            pltpu.VMEM((s // tq, tq, tq), F32),
            pltpu.VMEM((tq, LANES), F32),
        ],
        compiler_params=_params("arbitrary"),
        name="attention",
    )(q, k, v, v)


def _oproj_kernel(o_ref, olast_ref, x_ref, gate_ref, wo_ref, out_ref, cat_scr):
    last = pl.program_id(1) == pl.num_programs(1) - 1
    for hd in range(N_HEADS):
        cat_scr[:, hd * V_HEAD_DIM:(hd + 1) * V_HEAD_DIM] = jnp.where(last, olast_ref[0, hd], o_ref[0, hd])
    out_ref[0] = x_ref[0] + gate_ref[0] * _dot(cat_scr[...], wo_ref[...])


def _out_proj(o, o_last, x, gate, w_o):
    b, s, d = x.shape
    tm = o_last.shape[2]
    return pl.pallas_call(
        _oproj_kernel,
        grid=(b, s // tm),
        in_specs=[
            pl.BlockSpec((1, N_HEADS, tm, V_HEAD_DIM), lambda i, t: (i, 0, t, 0)),
            pl.BlockSpec((1, N_HEADS, tm, V_HEAD_DIM), lambda i, t: (i, 0, 0, 0)),
            pl.BlockSpec((1, tm, d), lambda i, t: (i, t, 0)),
            pl.BlockSpec((1, 1, d), lambda i, t: (i, 0, 0)),
            pl.BlockSpec(w_o.shape, lambda i, t: (0, 0)),
        ],
        out_specs=pl.BlockSpec((1, tm, d), lambda i, t: (i, t, 0)),
        out_shape=jax.ShapeDtypeStruct((b, s, d), F32),
        scratch_shapes=[pltpu.VMEM((tm, N_HEADS * V_HEAD_DIM), BF16)],
        compiler_params=_params("parallel", "parallel"),
        name="out_proj",
    )(o, o_last, x, gate, w_o)


ROW_TILE = 512
FFN_ROWS = 1024
Q_ROWS = 1024
FFN_DOWN_COLS = 512
FFN_TILE = 512
MOD_TILE = 256
CAST_ROWS = 256


def _swap_halves(w):
    half = w.shape[-1] // 2
    return jnp.concatenate([w[..., half:], w[..., :half]], axis=-1)


def _query_slots(w_uq):
    r = w_uq.shape[0]
    w = w_uq.reshape(r, N_HEADS, Q_HEAD_DIM)
    rope = w[..., QK_NOPE_DIM:]
    return jnp.concatenate([w[..., :QK_NOPE_DIM], rope, _swap_halves(rope)], axis=-1).reshape(
        r, N_HEADS * HEAD_SLOT)


def kernel(x, c, positions, mod_w, mod_b, norm_mix, norm_ffn, pool_w, pool_scale, kv_mod_w, kv_mod_b,
           kv_in_norm, w_dkv, kv_norm, w_uk, w_uv, w_kr, w_dq, q_norm, w_uq, w_o, ffn_gate, ffn_up,
           ffn_down, final_norm):
    b, s, d = x.shape

    c_rows = jnp.zeros((8, d), F32).at[:b].set(c)
    mod = _modulation(c_rows, mod_w, mod_b[:, None, :], MOD_TILE)[:, :b]
    kv_mod = _modulation(c_rows, kv_mod_w[None], kv_mod_b[None, None, :], MOD_TILE)[0, :b]
    table = _rope_table(positions, ROW_TILE)

    def vecs(a, n):
        return [v[:, None, :] for v in jnp.split(a, n, axis=-1)]

    wg = _cast_retile(ffn_gate, FFN_TILE, CAST_ROWS)
    wu = _cast_retile(ffn_up, FFN_TILE, CAST_ROWS)
    wd = _cast_retile(ffn_down, d, FFN_TILE).reshape(ffn_down.shape)

    kv = None
    for i in range(DEPTH):
        shift_m, scale_m, gate_m, shift_f, scale_f, gate_f = vecs(mod[i], N_MOD)
        if i == N_A:
            kv_shift, kv_scale = vecs(kv_mod, 2)
            w_kr2 = jnp.concatenate([w_kr, _swap_halves(w_kr)], axis=-1).astype(BF16)
            kv = _shared_kv(x, kv_shift, kv_scale, kv_in_norm[None], w_dkv.astype(BF16), kv_norm[None],
                            w_uk.astype(BF16), w_uv.astype(BF16), w_kr2, table, ROW_TILE)
        if i < N_A:
            x = _pool_layer(x, shift_m, scale_m, gate_m, norm_mix[i][None], pool_w[i].astype(BF16),
                            pool_scale[i][None], ROW_TILE)
        else:
            j = i - N_A
            q = _queries(x, shift_m, scale_m, norm_mix[i][None], w_dq[j].astype(BF16), q_norm[j][None],
                         _query_slots(w_uq[j]).astype(BF16), table, Q_ROWS)
            o, o_last = _attention(q, kv[0], kv[1], ROW_TILE)
            x = _out_proj(o, o_last, x, gate_m, w_o[j].astype(BF16))
        x = _ffn_layer(x, shift_f, scale_f, gate_f, norm_ffn[i][None], wg, wu, wd, final_norm[None],
                       layer=i, tm=FFN_ROWS, final_norm=(i == DEPTH - 1))
    return x
```

```python
import functools
import math

import jax
import jax.numpy as jnp
from jax import lax
from jax.experimental import pallas as pl
from jax.experimental.pallas import tpu as pltpu

D_MODEL = 2048
DEPTH = 4
N_A = DEPTH // 2
POOL_WINDOWS = (2, 4, 8, 16)
N_POOL_GROUPS = len(POOL_WINDOWS)
POOL_GROUP_DIM = D_MODEL // N_POOL_GROUPS
POOL_HALO = 16
QK_NOPE_DIM = 128
QK_ROPE_DIM = 64
V_HEAD_DIM = 128
N_HEADS = D_MODEL // V_HEAD_DIM
Q_HEAD_DIM = QK_NOPE_DIM + QK_ROPE_DIM
HEAD_SLOT = 256
SM_SCALE = Q_HEAD_DIM ** -0.5
ROPE_THETA = 10000.0
N_MOD = 6
EPS = 1e-6
LOG2E = math.log2(math.e)
NEG_BIG = -1e30

LANES = 128
VMEM_LIMIT = 60000 * 1024

BF16 = jnp.bfloat16
F32 = jnp.float32


def _params(*semantics):
    return pltpu.CompilerParams(dimension_semantics=semantics, vmem_limit_bytes=VMEM_LIMIT)


def _dot(a, b):
    return jnp.dot(a, b, preferred_element_type=F32)


def _norm_mod(x, gain, shift):
    ms = jnp.mean(x * x, axis=-1, keepdims=True)
    return x * lax.rsqrt(ms + EPS) * gain + shift


def _rmsnorm(x, g):
    ms = jnp.mean(x * x, axis=-1, keepdims=True)
    return x * lax.rsqrt(ms + EPS) * g


NORM_ROWS = 16


def _mod_kernel(c_ref, w_ref, b_ref, o_ref):
    k = pl.program_id(1)

    @pl.when(k == 0)
    def _():
        o_ref[...] = jnp.broadcast_to(b_ref[...], o_ref.shape)

    c = c_ref[...]
    sc = (c * jax.nn.sigmoid(c)).astype(BF16)
    o_ref[...] += _dot(sc, w_ref[...].astype(BF16))


def _modulation(c_rows, w, b, tk):
    n_layers, d, n = w.shape
    rows = c_rows.shape[0]
    return pl.pallas_call(
        _mod_kernel,
        grid=(n_layers, d // tk),
        in_specs=[
            pl.BlockSpec((rows, tk), lambda l, k: (0, k)),
            pl.BlockSpec((None, tk, n), lambda l, k: (l, k, 0)),
            pl.BlockSpec((None, 1, n), lambda l, k: (l, 0, 0)),
        ],
        out_specs=pl.BlockSpec((None, rows, n), lambda l, k: (l, 0, 0)),
        out_shape=jax.ShapeDtypeStruct((n_layers, rows, n), F32),
        compiler_params=_params("parallel", "arbitrary"),
        name="modulation",
    )(c_rows, w, b)


def _rope_table_kernel(pos_ref, freq_ref, o_ref):
    ang = pos_ref[0].astype(F32) * freq_ref[...]
    lane = lax.broadcasted_iota(jnp.int32, ang.shape, 1)
    sin = jnp.sin(ang)
    minus_sin_end = QK_ROPE_DIM + QK_ROPE_DIM // 2
    o_ref[0] = jnp.where(lane < QK_ROPE_DIM, jnp.cos(ang), jnp.where(lane < minus_sin_end, -sin, sin))


def _rope_table(positions, ts):
    b, s = positions.shape
    half = QK_ROPE_DIM // 2
    inv_freq = 1.0 / (ROPE_THETA ** (jnp.arange(0, QK_ROPE_DIM, 2, dtype=F32) / QK_ROPE_DIM))
    freq = jnp.tile(inv_freq, LANES // half)[None, :]
    return pl.pallas_call(
        _rope_table_kernel,
        grid=(b, s // ts),
        in_specs=[
            pl.BlockSpec((1, ts, 1), lambda i, t: (i, t, 0)),
            pl.BlockSpec((1, LANES), lambda i, t: (0, 0)),
        ],
        out_specs=pl.BlockSpec((1, ts, LANES), lambda i, t: (i, t, 0)),
        out_shape=jax.ShapeDtypeStruct((b, s, LANES), F32),
        compiler_params=_params("parallel", "parallel"),
        name="rope_table",
    )(positions.reshape(b, s, 1), freq)


def _apply_rope(t2, table):
    prod = t2 * table
    return prod + pltpu.roll(prod, QK_ROPE_DIM, axis=1)


def _pool_kernel(x_ref, halo_ref, shift_ref, scale_ref, gate_ref, g_ref, w_ref, ps_ref, o_ref):
    t = pl.program_id(1)
    ts = x_ref.shape[1]
    x = x_ref[0]
    gain = g_ref[...] * (1.0 + scale_ref[0])
    shift = shift_ref[0]
    h = _norm_mod(x, gain, shift)
    hh = _norm_mod(halo_ref[0], gain, shift) * (t > 0).astype(F32)
    hs = jnp.concatenate([hh, h], axis=0)
    seen = (t * ts + 1 + lax.broadcasted_iota(jnp.int32, (ts, LANES), 0)).astype(F32)
    out_scale = gate_ref[0] * ps_ref[...]
    for g, w in enumerate(POOL_WINDOWS):
        lo, hi = g * POOL_GROUP_DIM, (g + 1) * POOL_GROUP_DIM
        s = hs[:, lo:hi]
        k = 1
        while k < w:
            s = s + pltpu.roll(s, k, axis=0)
            k *= 2
        inv_cnt = 1.0 / jnp.minimum(seen, float(w))
        d = jnp.concatenate(
            [s[POOL_HALO:, c:c + LANES] * inv_cnt - h[:, lo + c:lo + c + LANES]
             for c in range(0, POOL_GROUP_DIM, LANES)], axis=1)
        y = _dot(d.astype(BF16), w_ref[g])
        o_ref[0, :, lo:hi] = x[:, lo:hi] + out_scale[:, lo:hi] * y


def _pool_layer(x, shift, scale, gate, norm_w, pool_w, pool_scale, ts):
    b, s, d = x.shape
    halo_blocks = ts // POOL_HALO
    vec = pl.BlockSpec((1, 1, d), lambda i, t: (i, 0, 0))
    row = pl.BlockSpec((1, d), lambda i, t: (0, 0))
    return pl.pallas_call(
        _pool_kernel,
        grid=(b, s // ts),
        in_specs=[
            pl.BlockSpec((1, ts, d), lambda i, t: (i, t, 0)),
            pl.BlockSpec((1, POOL_HALO, d), lambda i, t: (i, jnp.maximum(t * halo_blocks - 1, 0), 0)),
            vec, vec, vec, row,
            pl.BlockSpec((N_POOL_GROUPS, POOL_GROUP_DIM, POOL_GROUP_DIM), lambda i, t: (0, 0, 0)),
            row,
        ],
        out_specs=pl.BlockSpec((1, ts, d), lambda i, t: (i, t, 0)),
        out_shape=jax.ShapeDtypeStruct((b, s, d), F32),
        compiler_params=_params("parallel", "parallel"),
        name="pool_layer",
    )(x, x, shift, scale, gate, norm_w, pool_w, pool_scale)


def _ffn_kernel(x_hbm, shift_ref, scale_ref, gate_ref, g_ref, wg_ref, wu_ref, wd_ref, fn_ref,
                o_ref, xbuf, h_scr, sem, *, final_norm):
    n, f = pl.program_id(0), pl.program_id(1)
    tm, d = o_ref.shape[1:]
    row_chunks = [slice(r, r + SUB_ROWS) for r in range(0, tm, SUB_ROWS)]

    def fetch(tile):
        return pltpu.make_async_copy(x_hbm.at[tile], xbuf, sem)

    @pl.when(jnp.logical_and(n == 0, f == 0))
    def _():
        fetch(0).start()

    @pl.when(f == 0)
    def _():
        fetch(n).wait()
        gain = g_ref[...] * (1.0 + scale_ref[0])
        shift = shift_ref[0]
        for r in range(0, tm, NORM_ROWS):
            rows = slice(r, r + NORM_ROWS)
            x = xbuf[rows, :]
            h_scr[rows, :] = _norm_mod(x, gain, shift).astype(BF16)
            o_ref[0, rows, :] = x

    @pl.when(jnp.logical_and(f == 1, n + 1 < pl.num_programs(0)))
    def _():
        fetch(n + 1).start()

    h = h_scr[...]
    tf = wg_ref.shape[1]
    acts = []
    for c0 in range(0, tf, tf // 2):
        a = _dot(h, wg_ref[:, c0:c0 + tf // 2])
        u = _dot(h, wu_ref[:, c0:c0 + tf // 2])
        acts.append((a * jax.nn.sigmoid(a) * u).astype(BF16))
    gate = gate_ref[0]
    for c0 in range(0, d, FFN_DOWN_COLS):
        cols = slice(c0, c0 + FFN_DOWN_COLS)
        part = _dot(acts[0], wd_ref[:tf // 2, cols]) + _dot(acts[1], wd_ref[tf // 2:, cols])
        o_ref[0, :, cols] += gate[:, cols] * part

    if final_norm:
        @pl.when(f == pl.num_programs(1) - 1)
        def _():
            for rows in row_chunks:
                o_ref[0, rows, :] = _rmsnorm(o_ref[0, rows, :], fn_ref[...])


def _retile_kernel(w_ref, o_ref):
    tc = o_ref.shape[2]
    for j in range(o_ref.shape[0]):
        o_ref[j] = w_ref[:, j * tc:(j + 1) * tc].astype(o_ref.dtype)


def _cast_retile(w, tc, tr):
    n_layers, r, c = w.shape
    return pl.pallas_call(
        _retile_kernel,
        grid=(n_layers, r // tr),
        in_specs=[pl.BlockSpec((None, tr, c), lambda l, i: (l, i, 0))],
        out_specs=pl.BlockSpec((None, c // tc, tr, tc), lambda l, i: (l, 0, i, 0)),
        out_shape=jax.ShapeDtypeStruct((n_layers, c // tc, r, tc), BF16),
        compiler_params=_params("parallel", "parallel"),
        name="cast_retile",
    )(w)


def _ffn_layer(x, shift, scale, gate, norm_w, wg, wu, wd, final_w, layer, tm, final_norm):
    b, s, d = x.shape
    _, nf, _, tf = wg.shape
    nt = s // tm
    vec = pl.BlockSpec((1, 1, d), lambda n, f: (n // nt, 0, 0))
    row = pl.BlockSpec((1, d), lambda n, f: (0, 0))
    return pl.pallas_call(
        functools.partial(_ffn_kernel, final_norm=final_norm),
        grid=(b * nt, nf),
        in_specs=[
            pl.BlockSpec(memory_space=pl.ANY),
            vec, vec, vec, row,
            pl.BlockSpec((None, None, d, tf), lambda n, f: (layer, f, 0, 0)),
            pl.BlockSpec((None, None, d, tf), lambda n, f: (layer, f, 0, 0)),
            pl.BlockSpec((None, tf, d), lambda n, f: (layer, f, 0)),
            row,
        ],
        out_specs=pl.BlockSpec((1, tm, d), lambda n, f: (n // nt, n % nt, 0)),
        out_shape=jax.ShapeDtypeStruct((b, s, d), F32),
        scratch_shapes=[pltpu.VMEM((tm, d), F32), pltpu.VMEM((tm, d), BF16), pltpu.SemaphoreType.DMA(())],
        compiler_params=_params("arbitrary", "arbitrary"),
        name="ffn_layer",
    )(x.reshape(b * nt, tm, d), shift, scale, gate, norm_w, wg, wu, wd, final_w)


SUB_ROWS = 256


def _kv_kernel(x_ref, shift_ref, scale_ref, g_ref, wdkv_ref, kvn_ref, wuk_ref, wuv_ref, wkr_ref,
               tab_ref, k_ref, v_ref):
    gain = g_ref[...] * (1.0 + scale_ref[0])
    shift = shift_ref[0]
    chains = [slice(r0, r0 + SUB_ROWS) for r0 in range(0, x_ref.shape[1], SUB_ROWS)]
    hs = [_norm_mod(x_ref[0, rows, :], gain, shift).astype(BF16) for rows in chains]
    cs = [_dot(h, wdkv_ref[...]) for h in hs]
    krs = [_dot(h, wkr_ref[...]) for h in hs]
    ckvs = [_rmsnorm(c, kvn_ref[...]).astype(BF16) for c in cs]
    for rows, ckv, kr in zip(chains, ckvs, krs):
        kn = _dot(ckv, wuk_ref[...]).astype(BF16)
        v = _dot(ckv, wuv_ref[...]).astype(BF16)
        kr = _apply_rope(kr, tab_ref[0, rows, :])
        lane = lax.broadcasted_iota(jnp.int32, kr.shape, 1)
        kr = jnp.where(lane < QK_ROPE_DIM, kr, 0.0).astype(BF16)
        for hd in range(N_HEADS):
            k_ref[0, hd, rows, :QK_NOPE_DIM] = kn[:, hd * QK_NOPE_DIM:(hd + 1) * QK_NOPE_DIM]
            k_ref[0, hd, rows, QK_NOPE_DIM:] = kr
            v_ref[0, hd, rows, :] = v[:, hd * V_HEAD_DIM:(hd + 1) * V_HEAD_DIM]


def _shared_kv(x, shift, scale, norm_w, w_dkv, kv_norm, w_uk, w_uv, w_kr2, table, ts):
    b, s, d = x.shape
    full = lambda a: pl.BlockSpec(a.shape, lambda i, t: (0,) * a.ndim)
    vec = pl.BlockSpec((1, 1, d), lambda i, t: (i, 0, 0))
    return pl.pallas_call(
        _kv_kernel,
        grid=(b, s // ts),
        in_specs=[
            pl.BlockSpec((1, ts, d), lambda i, t: (i, t, 0)),
            vec, vec, full(norm_w), full(w_dkv), full(kv_norm), full(w_uk), full(w_uv), full(w_kr2),
            pl.BlockSpec((1, ts, LANES), lambda i, t: (i, t, 0)),
        ],
        out_specs=[
            pl.BlockSpec((1, N_HEADS, ts, HEAD_SLOT), lambda i, t: (i, 0, t, 0)),
            pl.BlockSpec((1, N_HEADS, ts, V_HEAD_DIM), lambda i, t: (i, 0, t, 0)),
        ],
        out_shape=[
            jax.ShapeDtypeStruct((b, N_HEADS, s, HEAD_SLOT), BF16),
            jax.ShapeDtypeStruct((b, N_HEADS, s, V_HEAD_DIM), BF16),
        ],
        compiler_params=_params("parallel", "parallel"),
        name="shared_kv",
    )(x, shift, scale, norm_w, w_dkv, kv_norm, w_uk, w_uv, w_kr2, table)


def _q_kernel(x_ref, shift_ref, scale_ref, g_ref, wdq_ref, qn_ref, wuq_ref, tab_ref, q_ref):
    gain = g_ref[...] * (1.0 + scale_ref[0])
    h = _norm_mod(x_ref[0], gain, shift_ref[0]).astype(BF16)
    cq = _rmsnorm(_dot(h, wdq_ref[...]), qn_ref[...]).astype(BF16)
    table = tab_ref[0] * (SM_SCALE * LOG2E)
    for hd in range(N_HEADS):
        q = _dot(cq, wuq_ref[:, hd * HEAD_SLOT:(hd + 1) * HEAD_SLOT])
        q_ref[0, hd, :, :QK_NOPE_DIM] = (q[:, :QK_NOPE_DIM] * (SM_SCALE * LOG2E)).astype(BF16)
        q_ref[0, hd, :, QK_NOPE_DIM:] = _apply_rope(q[:, QK_NOPE_DIM:], table).astype(BF16)


def _queries(x, shift, scale, norm_w, w_dq, q_norm, w_uq_slots, table, ts):
    b, s, d = x.shape
    full = lambda a: pl.BlockSpec(a.shape, lambda i, t: (0,) * a.ndim)
    vec = pl.BlockSpec((1, 1, d), lambda i, t: (i, 0, 0))
    return pl.pallas_call(
        _q_kernel,
        grid=(b, s // ts),
        in_specs=[
            pl.BlockSpec((1, ts, d), lambda i, t: (i, t, 0)),
            vec, vec, full(norm_w), full(w_dq), full(q_norm), full(w_uq_slots),
            pl.BlockSpec((1, ts, LANES), lambda i, t: (i, t, 0)),
        ],
        out_specs=pl.BlockSpec((1, N_HEADS, ts, HEAD_SLOT), lambda i, t: (i, 0, t, 0)),
        out_shape=jax.ShapeDtypeStruct((b, N_HEADS, s, HEAD_SLOT), BF16),
        compiler_params=_params("parallel", "parallel"),
        name="queries",
    )(x, shift, scale, norm_w, w_dq, q_norm, w_uq_slots, table)


def _lane_fold(x, op):
    r = x[:, :LANES]
    for c in range(1, x.shape[1] // LANES):
        r = op(r, x[:, c * LANES:(c + 1) * LANES])
    return r


def _attn_kernel(q_ref, k_ref, v_ref, vprev_ref, o_ref, olast_ref, s_scr, slast_scr, mlast_scr, *, tq):
    seq = q_ref.shape[2]
    nq = seq // tq
    lane = lax.broadcasted_iota(jnp.int32, (tq, LANES), 1)
    ones_col = jnp.where(lane == 0, 1.0, 0.0).astype(BF16)

    half = tq // 2

    def probs(s, mb):
        ps = [jnp.exp2(s[:, c * LANES:(c + 1) * LANES] - mb) for c in range(s.shape[1] // LANES)]
        return jnp.concatenate(ps, axis=1).astype(BF16)

    def pv_pass(scores, mb, values, n_kv):
        acc = None
        for j in range(n_kv):
            vv = jnp.concatenate([values[0, 0, j * tq:(j + 1) * tq, :], ones_col], axis=1)
            if j < n_kv - 1:
                d = _dot(probs(scores[j], mb), vv)
            else:
                d = _dot(probs(scores[j, :, :half], mb), vv[:half])
                d_low = _dot(probs(scores[j, half:, half:], mb[half:]), vv[half:])
                d = jnp.concatenate([d[:half], d[half:] + d_low], axis=0)
            acc = d if acc is None else acc + d
        return (acc[:, :V_HEAD_DIM] / acc[:, V_HEAD_DIM:V_HEAD_DIM + 1]).astype(o_ref.dtype)

    def causal(s):
        r = lax.broadcasted_iota(jnp.int32, s.shape, 0)
        c = lax.broadcasted_iota(jnp.int32, s.shape, 1)
        return jnp.where(c <= r, s, NEG_BIG)

    def qk(q, k):
        return lax.dot_general(q, k, (((1,), (1,)), ((), ())), preferred_element_type=F32)

    @pl.when(pl.program_id(0) == 0)
    def _():
        slast_scr[...] = jnp.zeros(slast_scr.shape, F32)
        mlast_scr[...] = jnp.zeros(mlast_scr.shape, F32)

    olast_ref[0, 0] = pv_pass(slast_scr, mlast_scr[...], vprev_ref, nq)

    for t in range(nq):
        rows = slice(t * tq, (t + 1) * tq)
        q = q_ref[0, 0, rows, :]
        scores = s_scr if t < nq - 1 else slast_scr
        m_run = None
        for j in range(t):
            s = qk(q, k_ref[0, 0, j * tq:(j + 1) * tq, :])
            scores[j] = s
            f = _lane_fold(s, jnp.maximum)
            m_run = f if m_run is None else jnp.maximum(m_run, f)
        s_left = causal(qk(q, k_ref[0, 0, t * tq:t * tq + half, :]))
        s_low = causal(qk(q[half:], k_ref[0, 0, t * tq + half:(t + 1) * tq, :]))
        scores[t, :, :half] = s_left
        scores[t, half:, half:] = s_low
        f = _lane_fold(s_left, jnp.maximum)
        f = jnp.concatenate([f[:half], jnp.maximum(f[half:], _lane_fold(s_low, jnp.maximum))], axis=0)
        m_run = f if m_run is None else jnp.maximum(m_run, f)
        mb = jnp.broadcast_to(jnp.max(m_run, axis=-1, keepdims=True), (tq, LANES))
        if t < nq - 1:
            o_ref[0, 0, rows, :] = pv_pass(s_scr, mb, v_ref, t + 1)
        else:
            mlast_scr[...] = mb
            o_ref[0, 0, rows, :] = jnp.zeros((tq, V_HEAD_DIM), o_ref.dtype)


def _attention(q, k, v, tq):
    b, nh, s, _ = q.shape
    n_heads_total = b * nh

    def cur(n):
        m = jnp.minimum(n, n_heads_total - 1)
        return (m // nh, m % nh, 0, 0)

    def prev(n):
        m = jnp.maximum(n - 1, 0)
        return (m // nh, m % nh, 0, 0)

    return pl.pallas_call(
        functools.partial(_attn_kernel, tq=tq),
        grid=(n_heads_total + 1,),
        in_specs=[
            pl.BlockSpec((1, 1, s, HEAD_SLOT), cur),
            pl.BlockSpec((1, 1, s, HEAD_SLOT), cur),
            pl.BlockSpec((1, 1, s, V_HEAD_DIM), cur),
            pl.BlockSpec((1, 1, s, V_HEAD_DIM), prev),
        ],
        out_specs=[
            pl.BlockSpec((1, 1, s, V_HEAD_DIM), cur),
            pl.BlockSpec((1, 1, tq, V_HEAD_DIM), prev),
        ],
        out_shape=[
            jax.ShapeDtypeStruct((b, nh, s, V_HEAD_DIM), BF16),
            jax.ShapeDtypeStruct((b, nh, tq, V_HEAD_DIM), BF16),
        ],
        scratch_shapes=[
            pltpu.VMEM((s // tq - 1, tq, tq), F32),
            pltpu.VMEM((s // tq, tq, tq), F32),
            pltpu.VMEM((tq, LANES), F32),
        ],
        compiler_params=_params("arbitrary"),
        name="attention",
    )(q, k, v, v)


def _oproj_kernel(o_ref, olast_ref, x_ref, gate_ref, wo_ref, out_ref, cat_scr):
    last = pl.program_id(1) == pl.num_programs(1) - 1
    for hd in range(N_HEADS):
        cat_scr[:, hd * V_HEAD_DIM:(hd + 1) * V_HEAD_DIM] = jnp.where(last, olast_ref[0, hd], o_ref[0, hd])
    out_ref[0] = x_ref[0] + gate_ref[0] * _dot(cat_scr[...], wo_ref[...])


def _out_proj(o, o_last, x, gate, w_o):
    b, s, d = x.shape
    tm = o_last.shape[2]
    return pl.pallas_call(
        _oproj_kernel,
        grid=(b, s // tm),
        in_specs=[
            pl.BlockSpec((1, N_HEADS, tm, V_HEAD_DIM), lambda i, t: (i, 0, t, 0)),
            pl.BlockSpec((1, N_HEADS, tm, V_HEAD_DIM), lambda i, t: (i, 0, 0, 0)),
            pl.BlockSpec((1, tm, d), lambda i, t: (i, t, 0)),
            pl.BlockSpec((1, 1, d), lambda i, t: (i, 0, 0)),
            pl.BlockSpec(w_o.shape, lambda i, t: (0, 0)),
        ],
        out_specs=pl.BlockSpec((1, tm, d), lambda i, t: (i, t, 0)),
        out_shape=jax.ShapeDtypeStruct((b, s, d), F32),
        scratch_shapes=[pltpu.VMEM((tm, N_HEADS * V_HEAD_DIM), BF16)],
        compiler_params=_params("parallel", "parallel"),
        name="out_proj",
    )(o, o_last, x, gate, w_o)


ROW_TILE = 512
FFN_ROWS = 1024
Q_ROWS = 1024
POOL_ROWS = 1024
FFN_DOWN_COLS = 512
FFN_TILE = 512
MOD_TILE = 256
CAST_ROWS = 256


def _swap_halves(w):
    half = w.shape[-1] // 2
    return jnp.concatenate([w[..., half:], w[..., :half]], axis=-1)


def _query_slots(w_uq):
    r = w_uq.shape[0]
    w = w_uq.reshape(r, N_HEADS, Q_HEAD_DIM)
    rope = w[..., QK_NOPE_DIM:]
    return jnp.concatenate([w[..., :QK_NOPE_DIM], rope, _swap_halves(rope)], axis=-1).reshape(
        r, N_HEADS * HEAD_SLOT)


def kernel(x, c, positions, mod_w, mod_b, norm_mix, norm_ffn, pool_w, pool_scale, kv_mod_w, kv_mod_b,
           kv_in_norm, w_dkv, kv_norm, w_uk, w_uv, w_kr, w_dq, q_norm, w_uq, w_o, ffn_gate, ffn_up,
           ffn_down, final_norm):
    b, s, d = x.shape

    c_rows = jnp.zeros((8, d), F32).at[:b].set(c)
    mod = _modulation(c_rows, mod_w, mod_b[:, None, :], MOD_TILE)[:, :b]
    kv_mod = _modulation(c_rows, kv_mod_w[None], kv_mod_b[None, None, :], MOD_TILE)[0, :b]
    table = _rope_table(positions, ROW_TILE)

    def vecs(a, n):
        return [v[:, None, :] for v in jnp.split(a, n, axis=-1)]

    wg = _cast_retile(ffn_gate, FFN_TILE, CAST_ROWS)
    wu = _cast_retile(ffn_up, FFN_TILE, CAST_ROWS)
    wd = _cast_retile(ffn_down, d, FFN_TILE).reshape(ffn_down.shape)

    kv = None
    for i in range(DEPTH):
        shift_m, scale_m, gate_m, shift_f, scale_f, gate_f = vecs(mod[i], N_MOD)
        if i == N_A:
            kv_shift, kv_scale = vecs(kv_mod, 2)
            w_kr2 = jnp.concatenate([w_kr, _swap_halves(w_kr)], axis=-1).astype(BF16)
            kv = _shared_kv(x, kv_shift, kv_scale, kv_in_norm[None], w_dkv.astype(BF16), kv_norm[None],
                            w_uk.astype(BF16), w_uv.astype(BF16), w_kr2, table, ROW_TILE)
        if i < N_A:
            x = _pool_layer(x, shift_m, scale_m, gate_m, norm_mix[i][None], pool_w[i].astype(BF16),
                            pool_scale[i][None], POOL_ROWS)
        else:
            j = i - N_A
            q = _queries(x, shift_m, scale_m, norm_mix[i][None], w_dq[j].astype(BF16), q_norm[j][None],
                         _query_slots(w_uq[j]).astype(BF16), table, Q_ROWS)
            o, o_last = _attention(q, kv[0], kv[1], ROW_TILE)
            x = _out_proj(o, o_last, x, gate_m, w_o[j].astype(BF16))
        x = _ffn_layer(x, shift_f, scale_f, gate_f, norm_ffn[i][None], wg, wu, wd, final_norm[None],
                       layer=i, tm=FFN_ROWS, final_norm=(i == DEPTH - 1))
    return x
```

```python
import functools
import math

import jax
import jax.numpy as jnp
from jax import lax
from jax.experimental import pallas as pl
from jax.experimental.pallas import tpu as pltpu

D_MODEL = 2048
DEPTH = 4
N_A = DEPTH // 2
POOL_WINDOWS = (2, 4, 8, 16)
N_POOL_GROUPS = len(POOL_WINDOWS)
POOL_GROUP_DIM = D_MODEL // N_POOL_GROUPS
POOL_HALO = 16
QK_NOPE_DIM = 128
QK_ROPE_DIM = 64
V_HEAD_DIM = 128
N_HEADS = D_MODEL // V_HEAD_DIM
Q_HEAD_DIM = QK_NOPE_DIM + QK_ROPE_DIM
HEAD_SLOT = 256
SM_SCALE = Q_HEAD_DIM ** -0.5
ROPE_THETA = 10000.0
N_MOD = 6
EPS = 1e-6
LOG2E = math.log2(math.e)
NEG_BIG = -1e30

LANES = 128
VMEM_LIMIT = 60000 * 1024

BF16 = jnp.bfloat16
F32 = jnp.float32


def _params(*semantics):
    return pltpu.CompilerParams(dimension_semantics=semantics, vmem_limit_bytes=VMEM_LIMIT)


def _dot(a, b):
    return jnp.dot(a, b, preferred_element_type=F32)


def _norm_mod(x, gain, shift):
    ms = jnp.mean(x * x, axis=-1, keepdims=True)
    return x * lax.rsqrt(ms + EPS) * gain + shift


def _rmsnorm(x, g):
    ms = jnp.mean(x * x, axis=-1, keepdims=True)
    return x * lax.rsqrt(ms + EPS) * g


NORM_ROWS = 16


def _mod_kernel(c_ref, w_ref, b_ref, o_ref):
    k = pl.program_id(1)

    @pl.when(k == 0)
    def _():
        o_ref[...] = jnp.broadcast_to(b_ref[...], o_ref.shape)

    c = c_ref[...]
    sc = (c * jax.nn.sigmoid(c)).astype(BF16)
    o_ref[...] += _dot(sc, w_ref[...].astype(BF16))


def _modulation(c_rows, w, b, tk):
    n_layers, d, n = w.shape
    rows = c_rows.shape[0]
    return pl.pallas_call(
        _mod_kernel,
        grid=(n_layers, d // tk),
        in_specs=[
            pl.BlockSpec((rows, tk), lambda l, k: (0, k)),
            pl.BlockSpec((None, tk, n), lambda l, k: (l, k, 0)),
            pl.BlockSpec((None, 1, n), lambda l, k: (l, 0, 0)),
        ],
        out_specs=pl.BlockSpec((None, rows, n), lambda l, k: (l, 0, 0)),
        out_shape=jax.ShapeDtypeStruct((n_layers, rows, n), F32),
        compiler_params=_params("parallel", "arbitrary"),
        name="modulation",
    )(c_rows, w, b)


def _rope_table_kernel(pos_ref, freq_ref, o_ref):
    ang = pos_ref[0].astype(F32) * freq_ref[...]
    lane = lax.broadcasted_iota(jnp.int32, ang.shape, 1)
    sin = jnp.sin(ang)
    minus_sin_end = QK_ROPE_DIM + QK_ROPE_DIM // 2
    o_ref[0] = jnp.where(lane < QK_ROPE_DIM, jnp.cos(ang), jnp.where(lane < minus_sin_end, -sin, sin))


def _rope_table(positions, ts):
    b, s = positions.shape
    half = QK_ROPE_DIM // 2
    inv_freq = 1.0 / (ROPE_THETA ** (jnp.arange(0, QK_ROPE_DIM, 2, dtype=F32) / QK_ROPE_DIM))
    freq = jnp.tile(inv_freq, LANES // half)[None, :]
    return pl.pallas_call(
        _rope_table_kernel,
        grid=(b, s // ts),
        in_specs=[
            pl.BlockSpec((1, ts, 1), lambda i, t: (i, t, 0)),
            pl.BlockSpec((1, LANES), lambda i, t: (0, 0)),
        ],
        out_specs=pl.BlockSpec((1, ts, LANES), lambda i, t: (i, t, 0)),
        out_shape=jax.ShapeDtypeStruct((b, s, LANES), F32),
        compiler_params=_params("parallel", "parallel"),
        name="rope_table",
    )(positions.reshape(b, s, 1), freq)


def _apply_rope(t2, table):
    prod = t2 * table
    return prod + pltpu.roll(prod, QK_ROPE_DIM, axis=1)


def _pool_kernel(x_ref, halo_ref, shift_ref, scale_ref, gate_ref, g_ref, w_ref, ps_ref, o_ref):
    t = pl.program_id(1)
    ts = x_ref.shape[1]
    x = x_ref[0]
    gain = g_ref[...] * (1.0 + scale_ref[0])
    shift = shift_ref[0]
    h = _norm_mod(x, gain, shift)
    hh = _norm_mod(halo_ref[0], gain, shift) * (t > 0).astype(F32)
    hs = jnp.concatenate([hh, h], axis=0)
    seen = (t * ts + 1 + lax.broadcasted_iota(jnp.int32, (ts, LANES), 0)).astype(F32)
    out_scale = gate_ref[0] * ps_ref[...]
    for g, w in enumerate(POOL_WINDOWS):
        lo, hi = g * POOL_GROUP_DIM, (g + 1) * POOL_GROUP_DIM
        s = hs[:, lo:hi]
        k = 1
        while k < w:
            s = s + pltpu.roll(s, k, axis=0)
            k *= 2
        inv_cnt = 1.0 / jnp.minimum(seen, float(w))
        d = jnp.concatenate(
            [s[POOL_HALO:, c:c + LANES] * inv_cnt - h[:, lo + c:lo + c + LANES]
             for c in range(0, POOL_GROUP_DIM, LANES)], axis=1)
        y = _dot(d.astype(BF16), w_ref[g])
        o_ref[0, :, lo:hi] = x[:, lo:hi] + out_scale[:, lo:hi] * y


def _pool_layer(x, shift, scale, gate, norm_w, pool_w, pool_scale, ts):
    b, s, d = x.shape
    halo_blocks = ts // POOL_HALO
    vec = pl.BlockSpec((1, 1, d), lambda i, t: (i, 0, 0))
    row = pl.BlockSpec((1, d), lambda i, t: (0, 0))
    return pl.pallas_call(
        _pool_kernel,
        grid=(b, s // ts),
        in_specs=[
            pl.BlockSpec((1, ts, d), lambda i, t: (i, t, 0)),
            pl.BlockSpec((1, POOL_HALO, d), lambda i, t: (i, jnp.maximum(t * halo_blocks - 1, 0), 0)),
            vec, vec, vec, row,
            pl.BlockSpec((N_POOL_GROUPS, POOL_GROUP_DIM, POOL_GROUP_DIM), lambda i, t: (0, 0, 0)),
            row,
        ],
        out_specs=pl.BlockSpec((1, ts, d), lambda i, t: (i, t, 0)),
        out_shape=jax.ShapeDtypeStruct((b, s, d), F32),
        compiler_params=_params("parallel", "parallel"),
        name="pool_layer",
    )(x, x, shift, scale, gate, norm_w, pool_w, pool_scale)


def _ffn_kernel(x_hbm, shift_ref, scale_ref, gate_ref, g_ref, wg_ref, wu_ref, wd_ref, fn_ref, *rest,
                final_norm, cast_next):
    if cast_next:
        wgn_ref, wun_ref, wdn_ref, o_ref, wgo_ref, wuo_ref, wdo_ref, xbuf, h_scr, sem = rest
        wgo_ref[...] = wgn_ref[...].astype(BF16)
        wuo_ref[...] = wun_ref[...].astype(BF16)
        wdo_ref[...] = wdn_ref[...].astype(BF16)
    else:
        o_ref, xbuf, h_scr, sem = rest
    n, f = pl.program_id(0), pl.program_id(1)
    tm, d = o_ref.shape[1:]
    row_chunks = [slice(r, r + SUB_ROWS) for r in range(0, tm, SUB_ROWS)]

    def fetch(tile):
        return pltpu.make_async_copy(x_hbm.at[tile], xbuf, sem)

    @pl.when(jnp.logical_and(n == 0, f == 0))
    def _():
        fetch(0).start()

    @pl.when(f == 0)
    def _():
        fetch(n).wait()
        gain = g_ref[...] * (1.0 + scale_ref[0])
        shift = shift_ref[0]
        for r in range(0, tm, NORM_ROWS):
            rows = slice(r, r + NORM_ROWS)
            x = xbuf[rows, :]
            h_scr[rows, :] = _norm_mod(x, gain, shift).astype(BF16)
            o_ref[0, rows, :] = x

    @pl.when(jnp.logical_and(f == 1, n + 1 < pl.num_programs(0)))
    def _():
        fetch(n + 1).start()

    h = h_scr[...]
    tf = wg_ref.shape[1]
    acts = []
    for c0 in range(0, tf, tf // 2):
        a = _dot(h, wg_ref[:, c0:c0 + tf // 2])
        u = _dot(h, wu_ref[:, c0:c0 + tf // 2])
        acts.append((a * jax.nn.sigmoid(a) * u).astype(BF16))
    gate = gate_ref[0]
    for c0 in range(0, d, FFN_DOWN_COLS):
        cols = slice(c0, c0 + FFN_DOWN_COLS)
        part = _dot(acts[0], wd_ref[:tf // 2, cols]) + _dot(acts[1], wd_ref[tf // 2:, cols])
        o_ref[0, :, cols] += gate[:, cols] * part

    if final_norm:
        @pl.when(f == pl.num_programs(1) - 1)
        def _():
            for rows in row_chunks:
                o_ref[0, rows, :] = _rmsnorm(o_ref[0, rows, :], fn_ref[...])


def _retile_kernel(w_ref, o_ref):
    tc = o_ref.shape[2]
    for j in range(o_ref.shape[0]):
        o_ref[j] = w_ref[:, j * tc:(j + 1) * tc].astype(o_ref.dtype)


def _cast_retile(w, layer, tc, tr):
    _, r, c = w.shape
    return pl.pallas_call(
        _retile_kernel,
        grid=(r // tr,),
        in_specs=[pl.BlockSpec((None, tr, c), lambda i: (layer, i, 0))],
        out_specs=pl.BlockSpec((c // tc, tr, tc), lambda i: (0, i, 0)),
        out_shape=jax.ShapeDtypeStruct((c // tc, r, tc), BF16),
        compiler_params=_params("parallel"),
        name="cast_retile",
    )(w)


def _ffn_layer(x, shift, scale, gate, norm_w, wg, wu, wd, final_w, tm, final_norm, next_f32=None):
    b, s, d = x.shape
    nf, _, tf = wg.shape
    nt = s // tm
    n_tiles = b * nt
    vec = pl.BlockSpec((1, 1, d), lambda n, f: (n // nt, 0, 0))
    row = pl.BlockSpec((1, d), lambda n, f: (0, 0))
    in_specs = [
        pl.BlockSpec(memory_space=pl.ANY),
        vec, vec, vec, row,
        pl.BlockSpec((None, d, tf), lambda n, f: (f, 0, 0)),
        pl.BlockSpec((None, d, tf), lambda n, f: (f, 0, 0)),
        pl.BlockSpec((tf, d), lambda n, f: (f, 0)),
        row,
    ]
    out_specs = [pl.BlockSpec((1, tm, d), lambda n, f: (n // nt, n % nt, 0))]
    out_shape = [jax.ShapeDtypeStruct((b, s, d), F32)]
    args = [x.reshape(n_tiles, tm, d), shift, scale, gate, norm_w, wg, wu, wd, final_w]
    if next_f32 is not None:
        gate_f32, up_f32, down_f32, nxt = next_f32
        up_rows, down_rows = d // n_tiles, tf // n_tiles
        in_specs += [
            pl.BlockSpec((None, up_rows, tf), lambda n, f: (nxt, n, f)),
            pl.BlockSpec((None, up_rows, tf), lambda n, f: (nxt, n, f)),
            pl.BlockSpec((None, down_rows, d), lambda n, f: (nxt, f * n_tiles + n, 0)),
        ]
        out_specs += [
            pl.BlockSpec((None, up_rows, tf), lambda n, f: (f, n, 0)),
            pl.BlockSpec((None, up_rows, tf), lambda n, f: (f, n, 0)),
            pl.BlockSpec((down_rows, d), lambda n, f: (f * n_tiles + n, 0)),
        ]
        out_shape += [jax.ShapeDtypeStruct(wg.shape, BF16), jax.ShapeDtypeStruct(wu.shape, BF16),
                      jax.ShapeDtypeStruct(wd.shape, BF16)]
        args += [gate_f32, up_f32, down_f32]
    out = pl.pallas_call(
        functools.partial(_ffn_kernel, final_norm=final_norm, cast_next=next_f32 is not None),
        grid=(n_tiles, nf),
        in_specs=in_specs,
        out_specs=out_specs,
        out_shape=out_shape,
        scratch_shapes=[pltpu.VMEM((tm, d), F32), pltpu.VMEM((tm, d), BF16), pltpu.SemaphoreType.DMA(())],
        compiler_params=_params("arbitrary", "arbitrary"),
        name="ffn_layer",
    )(*args)
    return out if next_f32 is not None else out[0]


SUB_ROWS = 256


def _kv_kernel(x_ref, shift_ref, scale_ref, g_ref, wdkv_ref, kvn_ref, wuk_ref, wuv_ref, wkr_ref,
               tab_ref, k_ref, v_ref):
    gain = g_ref[...] * (1.0 + scale_ref[0])
    shift = shift_ref[0]
    chains = [slice(r0, r0 + SUB_ROWS) for r0 in range(0, x_ref.shape[1], SUB_ROWS)]
    hs = [_norm_mod(x_ref[0, rows, :], gain, shift).astype(BF16) for rows in chains]
    cs = [_dot(h, wdkv_ref[...]) for h in hs]
    krs = [_dot(h, wkr_ref[...]) for h in hs]
    ckvs = [_rmsnorm(c, kvn_ref[...]).astype(BF16) for c in cs]
    for rows, ckv, kr in zip(chains, ckvs, krs):
        kn = _dot(ckv, wuk_ref[...]).astype(BF16)
        v = _dot(ckv, wuv_ref[...]).astype(BF16)
        kr = _apply_rope(kr, tab_ref[0, rows, :])
        lane = lax.broadcasted_iota(jnp.int32, kr.shape, 1)
        kr = jnp.where(lane < QK_ROPE_DIM, kr, 0.0).astype(BF16)
        for hd in range(N_HEADS):
            k_ref[0, hd, rows, :QK_NOPE_DIM] = kn[:, hd * QK_NOPE_DIM:(hd + 1) * QK_NOPE_DIM]
            k_ref[0, hd, rows, QK_NOPE_DIM:] = kr
            v_ref[0, hd, rows, :] = v[:, hd * V_HEAD_DIM:(hd + 1) * V_HEAD_DIM]


def _shared_kv(x, shift, scale, norm_w, w_dkv, kv_norm, w_uk, w_uv, w_kr2, table, ts):
    b, s, d = x.shape
    full = lambda a: pl.BlockSpec(a.shape, lambda i, t: (0,) * a.ndim)
    vec = pl.BlockSpec((1, 1, d), lambda i, t: (i, 0, 0))
    return pl.pallas_call(
        _kv_kernel,
        grid=(b, s // ts),
        in_specs=[
            pl.BlockSpec((1, ts, d), lambda i, t: (i, t, 0)),
            vec, vec, full(norm_w), full(w_dkv), full(kv_norm), full(w_uk), full(w_uv), full(w_kr2),
            pl.BlockSpec((1, ts, LANES), lambda i, t: (i, t, 0)),
        ],
        out_specs=[
            pl.BlockSpec((1, N_HEADS, ts, HEAD_SLOT), lambda i, t: (i, 0, t, 0)),
            pl.BlockSpec((1, N_HEADS, ts, V_HEAD_DIM), lambda i, t: (i, 0, t, 0)),
        ],
        out_shape=[
            jax.ShapeDtypeStruct((b, N_HEADS, s, HEAD_SLOT), BF16),
            jax.ShapeDtypeStruct((b, N_HEADS, s, V_HEAD_DIM), BF16),
        ],
        compiler_params=_params("parallel", "parallel"),
        name="shared_kv",
    )(x, shift, scale, norm_w, w_dkv, kv_norm, w_uk, w_uv, w_kr2, table)


def _q_kernel(x_ref, shift_ref, scale_ref, g_ref, wdq_ref, qn_ref, wuq_ref, tab_ref, q_ref):
    gain = g_ref[...] * (1.0 + scale_ref[0])
    h = _norm_mod(x_ref[0], gain, shift_ref[0]).astype(BF16)
    cq = _rmsnorm(_dot(h, wdq_ref[...]), qn_ref[...]).astype(BF16)
    table = tab_ref[0] * (SM_SCALE * LOG2E)
    for hd in range(N_HEADS):
        q = _dot(cq, wuq_ref[:, hd * HEAD_SLOT:(hd + 1) * HEAD_SLOT])
        q_ref[0, hd, :, :QK_NOPE_DIM] = (q[:, :QK_NOPE_DIM] * (SM_SCALE * LOG2E)).astype(BF16)
        q_ref[0, hd, :, QK_NOPE_DIM:] = _apply_rope(q[:, QK_NOPE_DIM:], table).astype(BF16)


def _queries(x, shift, scale, norm_w, w_dq, q_norm, w_uq_slots, table, ts):
    b, s, d = x.shape
    full = lambda a: pl.BlockSpec(a.shape, lambda i, t: (0,) * a.ndim)
    vec = pl.BlockSpec((1, 1, d), lambda i, t: (i, 0, 0))
    return pl.pallas_call(
        _q_kernel,
        grid=(b, s // ts),
        in_specs=[
            pl.BlockSpec((1, ts, d), lambda i, t: (i, t, 0)),
            vec, vec, full(norm_w), full(w_dq), full(q_norm), full(w_uq_slots),
            pl.BlockSpec((1, ts, LANES), lambda i, t: (i, t, 0)),
        ],
        out_specs=pl.BlockSpec((1, N_HEADS, ts, HEAD_SLOT), lambda i, t: (i, 0, t, 0)),
        out_shape=jax.ShapeDtypeStruct((b, N_HEADS, s, HEAD_SLOT), BF16),
        compiler_params=_params("parallel", "parallel"),
        name="queries",
    )(x, shift, scale, norm_w, w_dq, q_norm, w_uq_slots, table)


def _lane_fold(x, op):
    r = x[:, :LANES]
    for c in range(1, x.shape[1] // LANES):
        r = op(r, x[:, c * LANES:(c + 1) * LANES])
    return r


def _attn_kernel(q_ref, k_ref, v_ref, vprev_ref, o_ref, olast_ref, s_scr, slast_scr, mlast_scr, *, tq):
    seq = q_ref.shape[2]
    nq = seq // tq
    lane = lax.broadcasted_iota(jnp.int32, (tq, LANES), 1)
    ones_col = jnp.where(lane == 0, 1.0, 0.0).astype(BF16)

    half = tq // 2

    def probs(s, mb):
        ps = [jnp.exp2(s[:, c * LANES:(c + 1) * LANES] - mb) for c in range(s.shape[1] // LANES)]
        return jnp.concatenate(ps, axis=1).astype(BF16)

    def pv_pass(scores, mb, values, n_kv):
        acc = None
        for j in range(n_kv):
            vv = jnp.concatenate([values[0, 0, j * tq:(j + 1) * tq, :], ones_col], axis=1)
            if j < n_kv - 1:
                d = _dot(probs(scores[j], mb), vv)
            else:
                d = _dot(probs(scores[j, :, :half], mb), vv[:half])
                d_low = _dot(probs(scores[j, half:, half:], mb[half:]), vv[half:])
                d = jnp.concatenate([d[:half], d[half:] + d_low], axis=0)
            acc = d if acc is None else acc + d
        return (acc[:, :V_HEAD_DIM] / acc[:, V_HEAD_DIM:V_HEAD_DIM + 1]).astype(o_ref.dtype)

    def causal(s):
        r = lax.broadcasted_iota(jnp.int32, s.shape, 0)
        c = lax.broadcasted_iota(jnp.int32, s.shape, 1)
        return jnp.where(c <= r, s, NEG_BIG)

    def qk(q, k):
        return lax.dot_general(q, k, (((1,), (1,)), ((), ())), preferred_element_type=F32)

    @pl.when(pl.program_id(0) == 0)
    def _():
        slast_scr[...] = jnp.zeros(slast_scr.shape, F32)
        mlast_scr[...] = jnp.zeros(mlast_scr.shape, F32)

    olast_ref[0, 0] = pv_pass(slast_scr, mlast_scr[...], vprev_ref, nq)

    for t in range(nq):
        rows = slice(t * tq, (t + 1) * tq)
        q = q_ref[0, 0, rows, :]
        scores = s_scr if t < nq - 1 else slast_scr
        m_run = None
        for j in range(t):
            s = qk(q, k_ref[0, 0, j * tq:(j + 1) * tq, :])
            scores[j] = s
            f = _lane_fold(s, jnp.maximum)
            m_run = f if m_run is None else jnp.maximum(m_run, f)
        s_left = causal(qk(q, k_ref[0, 0, t * tq:t * tq + half, :]))
        s_low = causal(qk(q[half:], k_ref[0, 0, t * tq + half:(t + 1) * tq, :]))
        scores[t, :, :half] = s_left
        scores[t, half:, half:] = s_low
        f = _lane_fold(s_left, jnp.maximum)
        f = jnp.concatenate([f[:half], jnp.maximum(f[half:], _lane_fold(s_low, jnp.maximum))], axis=0)
        m_run = f if m_run is None else jnp.maximum(m_run, f)
        mb = jnp.broadcast_to(jnp.max(m_run, axis=-1, keepdims=True), (tq, LANES))
        if t < nq - 1:
            o_ref[0, 0, rows, :] = pv_pass(s_scr, mb, v_ref, t + 1)
        else:
            mlast_scr[...] = mb
            o_ref[0, 0, rows, :] = jnp.zeros((tq, V_HEAD_DIM), o_ref.dtype)


def _attention(q, k, v, tq):
    b, nh, s, _ = q.shape
    n_heads_total = b * nh

    def cur(n):
        m = jnp.minimum(n, n_heads_total - 1)
        return (m // nh, m % nh, 0, 0)

    def prev(n):
        m = jnp.maximum(n - 1, 0)
        return (m // nh, m % nh, 0, 0)

    return pl.pallas_call(
        functools.partial(_attn_kernel, tq=tq),
        grid=(n_heads_total + 1,),
        in_specs=[
            pl.BlockSpec((1, 1, s, HEAD_SLOT), cur),
            pl.BlockSpec((1, 1, s, HEAD_SLOT), cur),
            pl.BlockSpec((1, 1, s, V_HEAD_DIM), cur),
            pl.BlockSpec((1, 1, s, V_HEAD_DIM), prev),
        ],
        out_specs=[
            pl.BlockSpec((1, 1, s, V_HEAD_DIM), cur),
            pl.BlockSpec((1, 1, tq, V_HEAD_DIM), prev),
        ],
        out_shape=[
            jax.ShapeDtypeStruct((b, nh, s, V_HEAD_DIM), BF16),
            jax.ShapeDtypeStruct((b, nh, tq, V_HEAD_DIM), BF16),
        ],
        scratch_shapes=[
            pltpu.VMEM((s // tq - 1, tq, tq), F32),
            pltpu.VMEM((s // tq, tq, tq), F32),
            pltpu.VMEM((tq, LANES), F32),
        ],
        compiler_params=_params("arbitrary"),
        name="attention",
    )(q, k, v, v)


def _oproj_kernel(o_ref, olast_ref, x_ref, gate_ref, wo_ref, out_ref, cat_scr):
    last = pl.program_id(1) == pl.num_programs(1) - 1
    for hd in range(N_HEADS):
        cat_scr[:, hd * V_HEAD_DIM:(hd + 1) * V_HEAD_DIM] = jnp.where(last, olast_ref[0, hd], o_ref[0, hd])
    out_ref[0] = x_ref[0] + gate_ref[0] * _dot(cat_scr[...], wo_ref[...])


def _out_proj(o, o_last, x, gate, w_o):
    b, s, d = x.shape
    tm = o_last.shape[2]
    return pl.pallas_call(
        _oproj_kernel,
        grid=(b, s // tm),
        in_specs=[
            pl.BlockSpec((1, N_HEADS, tm, V_HEAD_DIM), lambda i, t: (i, 0, t, 0)),
            pl.BlockSpec((1, N_HEADS, tm, V_HEAD_DIM), lambda i, t: (i, 0, 0, 0)),
            pl.BlockSpec((1, tm, d), lambda i, t: (i, t, 0)),
            pl.BlockSpec((1, 1, d), lambda i, t: (i, 0, 0)),
            pl.BlockSpec(w_o.shape, lambda i, t: (0, 0)),
        ],
        out_specs=pl.BlockSpec((1, tm, d), lambda i, t: (i, t, 0)),
        out_shape=jax.ShapeDtypeStruct((b, s, d), F32),
        scratch_shapes=[pltpu.VMEM((tm, N_HEADS * V_HEAD_DIM), BF16)],
        compiler_params=_params("parallel", "parallel"),
        name="out_proj",
    )(o, o_last, x, gate, w_o)


ROW_TILE = 512
FFN_ROWS = 1024
Q_ROWS = 1024
POOL_ROWS = 1024
FFN_DOWN_COLS = 512
FFN_TILE = 512
MOD_TILE = 256
CAST_ROWS = 256


def _swap_halves(w):
    half = w.shape[-1] // 2
    return jnp.concatenate([w[..., half:], w[..., :half]], axis=-1)


def _query_slots(w_uq):
    r = w_uq.shape[0]
    w = w_uq.reshape(r, N_HEADS, Q_HEAD_DIM)
    rope = w[..., QK_NOPE_DIM:]
    return jnp.concatenate([w[..., :QK_NOPE_DIM], rope, _swap_halves(rope)], axis=-1).reshape(
        r, N_HEADS * HEAD_SLOT)


def kernel(x, c, positions, mod_w, mod_b, norm_mix, norm_ffn, pool_w, pool_scale, kv_mod_w, kv_mod_b,
           kv_in_norm, w_dkv, kv_norm, w_uk, w_uv, w_kr, w_dq, q_norm, w_uq, w_o, ffn_gate, ffn_up,
           ffn_down, final_norm):
    b, s, d = x.shape

    c_rows = jnp.zeros((8, d), F32).at[:b].set(c)
    mod = _modulation(c_rows, mod_w, mod_b[:, None, :], MOD_TILE)[:, :b]
    kv_mod = _modulation(c_rows, kv_mod_w[None], kv_mod_b[None, None, :], MOD_TILE)[0, :b]
    table = _rope_table(positions, ROW_TILE)

    def vecs(a, n):
        return [v[:, None, :] for v in jnp.split(a, n, axis=-1)]

    wg = _cast_retile(ffn_gate, 0, FFN_TILE, CAST_ROWS)
    wu = _cast_retile(ffn_up, 0, FFN_TILE, CAST_ROWS)
    wd = _cast_retile(ffn_down, 0, d, FFN_TILE).reshape(ffn_down.shape[1:])

    kv = None
    for i in range(DEPTH):
        shift_m, scale_m, gate_m, shift_f, scale_f, gate_f = vecs(mod[i], N_MOD)
        if i == N_A:
            kv_shift, kv_scale = vecs(kv_mod, 2)
            w_kr2 = jnp.concatenate([w_kr, _swap_halves(w_kr)], axis=-1).astype(BF16)
            kv = _shared_kv(x, kv_shift, kv_scale, kv_in_norm[None], w_dkv.astype(BF16), kv_norm[None],
                            w_uk.astype(BF16), w_uv.astype(BF16), w_kr2, table, ROW_TILE)
        if i < N_A:
            x = _pool_layer(x, shift_m, scale_m, gate_m, norm_mix[i][None], pool_w[i].astype(BF16),
                            pool_scale[i][None], POOL_ROWS)
        else:
            j = i - N_A
            q = _queries(x, shift_m, scale_m, norm_mix[i][None], w_dq[j].astype(BF16), q_norm[j][None],
                         _query_slots(w_uq[j]).astype(BF16), table, Q_ROWS)
            o, o_last = _attention(q, kv[0], kv[1], ROW_TILE)
            x = _out_proj(o, o_last, x, gate_m, w_o[j].astype(BF16))
        if i < DEPTH - 1:
            x, wg, wu, wd = _ffn_layer(x, shift_f, scale_f, gate_f, norm_ffn[i][None], wg, wu, wd,
                                       final_norm[None], tm=FFN_ROWS, final_norm=False,
                                       next_f32=(ffn_gate, ffn_up, ffn_down, i + 1))
        else:
            x = _ffn_layer(x, shift_f, scale_f, gate_f, norm_ffn[i][None], wg, wu, wd, final_norm[None],
                           tm=FFN_ROWS, final_norm=True)
    return x
```

```python
import functools
import math

import jax
import jax.numpy as jnp
from jax import lax
from jax.experimental import pallas as pl
from jax.experimental.pallas import tpu as pltpu

D_MODEL = 2048
DEPTH = 4
N_A = DEPTH // 2
POOL_WINDOWS = (2, 4, 8, 16)
N_POOL_GROUPS = len(POOL_WINDOWS)
POOL_GROUP_DIM = D_MODEL // N_POOL_GROUPS
POOL_HALO = 16
QK_NOPE_DIM = 128
QK_ROPE_DIM = 64
V_HEAD_DIM = 128
N_HEADS = D_MODEL // V_HEAD_DIM
Q_HEAD_DIM = QK_NOPE_DIM + QK_ROPE_DIM
HEAD_SLOT = 256
SM_SCALE = Q_HEAD_DIM ** -0.5
ROPE_THETA = 10000.0
N_MOD = 6
EPS = 1e-6
LOG2E = math.log2(math.e)
NEG_BIG = -1e30

LANES = 128
VMEM_LIMIT = 60000 * 1024

BF16 = jnp.bfloat16
F32 = jnp.float32


def _params(*semantics):
    return pltpu.CompilerParams(dimension_semantics=semantics, vmem_limit_bytes=VMEM_LIMIT)


def _dot(a, b):
    return jnp.dot(a, b, preferred_element_type=F32)


def _norm_mod(x, gain, shift):
    ms = jnp.mean(x * x, axis=-1, keepdims=True)
    return x * lax.rsqrt(ms + EPS) * gain + shift


def _rmsnorm(x, g):
    ms = jnp.mean(x * x, axis=-1, keepdims=True)
    return x * lax.rsqrt(ms + EPS) * g


NORM_ROWS = 16


def _mod_kernel(c_ref, w_ref, b_ref, o_ref):
    k = pl.program_id(1)

    @pl.when(k == 0)
    def _():
        o_ref[...] = jnp.broadcast_to(b_ref[...], o_ref.shape)

    c = c_ref[...]
    sc = (c * jax.nn.sigmoid(c)).astype(BF16)
    o_ref[...] += _dot(sc, w_ref[...].astype(BF16))


def _modulation(c_rows, w, b, tk):
    n_layers, d, n = w.shape
    rows = c_rows.shape[0]
    return pl.pallas_call(
        _mod_kernel,
        grid=(n_layers, d // tk),
        in_specs=[
            pl.BlockSpec((rows, tk), lambda l, k: (0, k)),
            pl.BlockSpec((None, tk, n), lambda l, k: (l, k, 0)),
            pl.BlockSpec((None, 1, n), lambda l, k: (l, 0, 0)),
        ],
        out_specs=pl.BlockSpec((None, rows, n), lambda l, k: (l, 0, 0)),
        out_shape=jax.ShapeDtypeStruct((n_layers, rows, n), F32),
        compiler_params=_params("parallel", "arbitrary"),
        name="modulation",
    )(c_rows, w, b)


def _rope_table_kernel(pos_ref, freq_ref, o_ref):
    ang = pos_ref[0].astype(F32) * freq_ref[...]
    lane = lax.broadcasted_iota(jnp.int32, ang.shape, 1)
    sin = jnp.sin(ang)
    minus_sin_end = QK_ROPE_DIM + QK_ROPE_DIM // 2
    o_ref[0] = jnp.where(lane < QK_ROPE_DIM, jnp.cos(ang), jnp.where(lane < minus_sin_end, -sin, sin))


def _rope_table(positions, ts):
    b, s = positions.shape
    half = QK_ROPE_DIM // 2
    inv_freq = 1.0 / (ROPE_THETA ** (jnp.arange(0, QK_ROPE_DIM, 2, dtype=F32) / QK_ROPE_DIM))
    freq = jnp.tile(inv_freq, LANES // half)[None, :]
    return pl.pallas_call(
        _rope_table_kernel,
        grid=(b, s // ts),
        in_specs=[
            pl.BlockSpec((1, ts, 1), lambda i, t: (i, t, 0)),
            pl.BlockSpec((1, LANES), lambda i, t: (0, 0)),
        ],
        out_specs=pl.BlockSpec((1, ts, LANES), lambda i, t: (i, t, 0)),
        out_shape=jax.ShapeDtypeStruct((b, s, LANES), F32),
        compiler_params=_params("parallel", "parallel"),
        name="rope_table",
    )(positions.reshape(b, s, 1), freq)


def _apply_rope(t2, table):
    prod = t2 * table
    return prod + pltpu.roll(prod, QK_ROPE_DIM, axis=1)


def _pool_kernel(x_ref, halo_ref, shift_ref, scale_ref, gate_ref, g_ref, w_ref, ps_ref, o_ref):
    t = pl.program_id(1)
    ts = x_ref.shape[1]
    x = x_ref[0]
    gain = g_ref[...] * (1.0 + scale_ref[0])
    shift = shift_ref[0]
    h = _norm_mod(x, gain, shift)
    hh = _norm_mod(halo_ref[0], gain, shift) * (t > 0).astype(F32)
    hs = jnp.concatenate([hh, h], axis=0)
    seen = (t * ts + 1 + lax.broadcasted_iota(jnp.int32, (ts, LANES), 0)).astype(F32)
    out_scale = gate_ref[0] * ps_ref[...]
    for g, w in enumerate(POOL_WINDOWS):
        lo, hi = g * POOL_GROUP_DIM, (g + 1) * POOL_GROUP_DIM
        s = hs[:, lo:hi]
        k = 1
        while k < w:
            s = s + pltpu.roll(s, k, axis=0)
            k *= 2
        inv_cnt = 1.0 / jnp.minimum(seen, float(w))
        d = jnp.concatenate(
            [s[POOL_HALO:, c:c + LANES] * inv_cnt - h[:, lo + c:lo + c + LANES]
             for c in range(0, POOL_GROUP_DIM, LANES)], axis=1)
        y = _dot(d.astype(BF16), w_ref[g])
        o_ref[0, :, lo:hi] = x[:, lo:hi] + out_scale[:, lo:hi] * y


def _pool_layer(x, shift, scale, gate, norm_w, pool_w, pool_scale, ts):
    b, s, d = x.shape
    halo_blocks = ts // POOL_HALO
    vec = pl.BlockSpec((1, 1, d), lambda i, t: (i, 0, 0))
    row = pl.BlockSpec((1, d), lambda i, t: (0, 0))
    return pl.pallas_call(
        _pool_kernel,
        grid=(b, s // ts),
        in_specs=[
            pl.BlockSpec((1, ts, d), lambda i, t: (i, t, 0)),
            pl.BlockSpec((1, POOL_HALO, d), lambda i, t: (i, jnp.maximum(t * halo_blocks - 1, 0), 0)),
            vec, vec, vec, row,
            pl.BlockSpec((N_POOL_GROUPS, POOL_GROUP_DIM, POOL_GROUP_DIM), lambda i, t: (0, 0, 0)),
            row,
        ],
        out_specs=pl.BlockSpec((1, ts, d), lambda i, t: (i, t, 0)),
        out_shape=jax.ShapeDtypeStruct((b, s, d), F32),
        compiler_params=_params("parallel", "parallel"),
        name="pool_layer",
    )(x, x, shift, scale, gate, norm_w, pool_w, pool_scale)


def _ffn_kernel(x_hbm, shift_ref, scale_ref, gate_ref, g_ref, wg_ref, wu_ref, wd_ref, fn_ref, *rest,
                final_norm, cast_next):
    if cast_next:
        wgn_ref, wun_ref, wdn_ref, o_ref, wgo_ref, wuo_ref, wdo_ref, xbuf, h_scr, sem = rest
        wgo_ref[...] = wgn_ref[...].astype(BF16)
        wuo_ref[...] = wun_ref[...].astype(BF16)
        wdo_ref[...] = wdn_ref[...].astype(BF16)
    else:
        o_ref, xbuf, h_scr, sem = rest
    n, f = pl.program_id(0), pl.program_id(1)
    tm, d = o_ref.shape[1:]
    row_chunks = [slice(r, r + SUB_ROWS) for r in range(0, tm, SUB_ROWS)]

    def fetch(tile):
        return pltpu.make_async_copy(x_hbm.at[tile], xbuf, sem)

    @pl.when(jnp.logical_and(n == 0, f == 0))
    def _():
        fetch(0).start()

    @pl.when(f == 0)
    def _():
        fetch(n).wait()
        gain = g_ref[...] * (1.0 + scale_ref[0])
        shift = shift_ref[0]
        for r in range(0, tm, NORM_ROWS):
            rows = slice(r, r + NORM_ROWS)
            x = xbuf[rows, :]
            h_scr[rows, :] = _norm_mod(x, gain, shift).astype(BF16)
            o_ref[0, rows, :] = x

    @pl.when(jnp.logical_and(f == 1, n + 1 < pl.num_programs(0)))
    def _():
        fetch(n + 1).start()

    h = h_scr[...]
    tf = wg_ref.shape[1]
    acts = []
    for c0 in range(0, tf, tf // 2):
        a = _dot(h, wg_ref[:, c0:c0 + tf // 2])
        u = _dot(h, wu_ref[:, c0:c0 + tf // 2])
        acts.append((a * jax.nn.sigmoid(a) * u).astype(BF16))
    gate = gate_ref[0]
    for c0 in range(0, d, FFN_DOWN_COLS):
        cols = slice(c0, c0 + FFN_DOWN_COLS)
        part = _dot(acts[0], wd_ref[:tf // 2, cols]) + _dot(acts[1], wd_ref[tf // 2:, cols])
        o_ref[0, :, cols] += gate[:, cols] * part

    if final_norm:
        @pl.when(f == pl.num_programs(1) - 1)
        def _():
            for rows in row_chunks:
                o_ref[0, rows, :] = _rmsnorm(o_ref[0, rows, :], fn_ref[...])


def _retile_kernel(w_ref, o_ref):
    tc = o_ref.shape[2]
    for j in range(o_ref.shape[0]):
        o_ref[j] = w_ref[:, j * tc:(j + 1) * tc].astype(o_ref.dtype)


def _cast_retile(w, layer, tc, tr):
    _, r, c = w.shape
    return pl.pallas_call(
        _retile_kernel,
        grid=(r // tr,),
        in_specs=[pl.BlockSpec((None, tr, c), lambda i: (layer, i, 0))],
        out_specs=pl.BlockSpec((c // tc, tr, tc), lambda i: (0, i, 0)),
        out_shape=jax.ShapeDtypeStruct((c // tc, r, tc), BF16),
        compiler_params=_params("parallel"),
        name="cast_retile",
    )(w)


def _ffn_layer(x, shift, scale, gate, norm_w, wg, wu, wd, final_w, tm, final_norm, next_f32=None):
    b, s, d = x.shape
    nf, _, tf = wg.shape
    nt = s // tm
    n_tiles = b * nt
    vec = pl.BlockSpec((1, 1, d), lambda n, f: (n // nt, 0, 0))
    row = pl.BlockSpec((1, d), lambda n, f: (0, 0))
    in_specs = [
        pl.BlockSpec(memory_space=pl.ANY),
        vec, vec, vec, row,
        pl.BlockSpec((None, d, tf), lambda n, f: (f, 0, 0)),
        pl.BlockSpec((None, d, tf), lambda n, f: (f, 0, 0)),
        pl.BlockSpec((tf, d), lambda n, f: (f, 0)),
        row,
    ]
    out_specs = [pl.BlockSpec((1, tm, d), lambda n, f: (n // nt, n % nt, 0))]
    out_shape = [jax.ShapeDtypeStruct((b, s, d), F32)]
    args = [x.reshape(n_tiles, tm, d), shift, scale, gate, norm_w, wg, wu, wd, final_w]
    if next_f32 is not None:
        gate_f32, up_f32, down_f32, nxt = next_f32
        up_rows, down_rows = d // n_tiles, tf // n_tiles
        in_specs += [
            pl.BlockSpec((None, up_rows, tf), lambda n, f: (nxt, n, f)),
            pl.BlockSpec((None, up_rows, tf), lambda n, f: (nxt, n, f)),
            pl.BlockSpec((None, down_rows, d), lambda n, f: (nxt, f * n_tiles + n, 0)),
        ]
        out_specs += [
            pl.BlockSpec((None, up_rows, tf), lambda n, f: (f, n, 0)),
            pl.BlockSpec((None, up_rows, tf), lambda n, f: (f, n, 0)),
            pl.BlockSpec((down_rows, d), lambda n, f: (f * n_tiles + n, 0)),
        ]
        out_shape += [jax.ShapeDtypeStruct(wg.shape, BF16), jax.ShapeDtypeStruct(wu.shape, BF16),
                      jax.ShapeDtypeStruct(wd.shape, BF16)]
        args += [gate_f32, up_f32, down_f32]
    out = pl.pallas_call(
        functools.partial(_ffn_kernel, final_norm=final_norm, cast_next=next_f32 is not None),
        grid=(n_tiles, nf),
        in_specs=in_specs,
        out_specs=out_specs,
        out_shape=out_shape,
        scratch_shapes=[pltpu.VMEM((tm, d), F32), pltpu.VMEM((tm, d), BF16), pltpu.SemaphoreType.DMA(())],
        compiler_params=_params("arbitrary", "arbitrary"),
        name="ffn_layer",
    )(*args)
    return out if next_f32 is not None else out[0]


SUB_ROWS = 256


def _kv_kernel(x_ref, shift_ref, scale_ref, g_ref, wdkv_ref, kvn_ref, wuk_ref, wuv_ref, wkr_ref,
               tab_ref, k_ref, v_ref):
    gain = g_ref[...] * (1.0 + scale_ref[0])
    shift = shift_ref[0]
    chains = [slice(r0, r0 + SUB_ROWS) for r0 in range(0, x_ref.shape[1], SUB_ROWS)]
    hs = [_norm_mod(x_ref[0, rows, :], gain, shift).astype(BF16) for rows in chains]
    cs = [_dot(h, wdkv_ref[...]) for h in hs]
    krs = [_dot(h, wkr_ref[...]) for h in hs]
    ckvs = [_rmsnorm(c, kvn_ref[...]).astype(BF16) for c in cs]
    for rows, ckv, kr in zip(chains, ckvs, krs):
        kn = _dot(ckv, wuk_ref[...]).astype(BF16)
        v = _dot(ckv, wuv_ref[...]).astype(BF16)
        kr = _apply_rope(kr, tab_ref[0, rows, :])
        lane = lax.broadcasted_iota(jnp.int32, kr.shape, 1)
        kr = jnp.where(lane < QK_ROPE_DIM, kr, 0.0).astype(BF16)
        for hd in range(N_HEADS):
            k_ref[0, hd, rows, :QK_NOPE_DIM] = kn[:, hd * QK_NOPE_DIM:(hd + 1) * QK_NOPE_DIM]
            k_ref[0, hd, rows, QK_NOPE_DIM:] = kr
            v_ref[0, hd, rows, :] = v[:, hd * V_HEAD_DIM:(hd + 1) * V_HEAD_DIM]


def _shared_kv(x, shift, scale, norm_w, w_dkv, kv_norm, w_uk, w_uv, w_kr2, table, ts):
    b, s, d = x.shape
    full = lambda a: pl.BlockSpec(a.shape, lambda i, t: (0,) * a.ndim, pipeline_mode=pl.Buffered(1))
    vec = pl.BlockSpec((1, 1, d), lambda i, t: (i, 0, 0))
    return pl.pallas_call(
        _kv_kernel,
        grid=(b, s // ts),
        in_specs=[
            pl.BlockSpec((1, ts, d), lambda i, t: (i, t, 0)),
            vec, vec, full(norm_w), full(w_dkv), full(kv_norm), full(w_uk), full(w_uv), full(w_kr2),
            pl.BlockSpec((1, ts, LANES), lambda i, t: (i, t, 0)),
        ],
        out_specs=[
            pl.BlockSpec((1, N_HEADS, ts, HEAD_SLOT), lambda i, t: (i, 0, t, 0)),
            pl.BlockSpec((1, N_HEADS, ts, V_HEAD_DIM), lambda i, t: (i, 0, t, 0)),
        ],
        out_shape=[
            jax.ShapeDtypeStruct((b, N_HEADS, s, HEAD_SLOT), BF16),
            jax.ShapeDtypeStruct((b, N_HEADS, s, V_HEAD_DIM), BF16),
        ],
        compiler_params=_params("parallel", "parallel"),
        name="shared_kv",
    )(x, shift, scale, norm_w, w_dkv, kv_norm, w_uk, w_uv, w_kr2, table)


def _q_kernel(x_ref, shift_ref, scale_ref, g_ref, wdq_ref, qn_ref, wuq_ref, tab_ref, q_ref):
    gain = g_ref[...] * (1.0 + scale_ref[0])
    h = _norm_mod(x_ref[0], gain, shift_ref[0]).astype(BF16)
    cq = _rmsnorm(_dot(h, wdq_ref[...]), qn_ref[...]).astype(BF16)
    table = tab_ref[0] * (SM_SCALE * LOG2E)
    for hd in range(N_HEADS):
        q = _dot(cq, wuq_ref[:, hd * HEAD_SLOT:(hd + 1) * HEAD_SLOT])
        q_ref[0, hd, :, :QK_NOPE_DIM] = (q[:, :QK_NOPE_DIM] * (SM_SCALE * LOG2E)).astype(BF16)
        q_ref[0, hd, :, QK_NOPE_DIM:] = _apply_rope(q[:, QK_NOPE_DIM:], table).astype(BF16)


def _queries(x, shift, scale, norm_w, w_dq, q_norm, w_uq_slots, table, ts):
    b, s, d = x.shape
    full = lambda a: pl.BlockSpec(a.shape, lambda i, t: (0,) * a.ndim)
    vec = pl.BlockSpec((1, 1, d), lambda i, t: (i, 0, 0))
    return pl.pallas_call(
        _q_kernel,
        grid=(b, s // ts),
        in_specs=[
            pl.BlockSpec((1, ts, d), lambda i, t: (i, t, 0)),
            vec, vec, full(norm_w), full(w_dq), full(q_norm), full(w_uq_slots),
            pl.BlockSpec((1, ts, LANES), lambda i, t: (i, t, 0)),
        ],
        out_specs=pl.BlockSpec((1, N_HEADS, ts, HEAD_SLOT), lambda i, t: (i, 0, t, 0)),
        out_shape=jax.ShapeDtypeStruct((b, N_HEADS, s, HEAD_SLOT), BF16),
        compiler_params=_params("parallel", "parallel"),
        name="queries",
    )(x, shift, scale, norm_w, w_dq, q_norm, w_uq_slots, table)


def _lane_fold(x, op):
    r = x[:, :LANES]
    for c in range(1, x.shape[1] // LANES):
        r = op(r, x[:, c * LANES:(c + 1) * LANES])
    return r


def _attn_kernel(q_ref, k_ref, v_ref, vprev_ref, o_ref, olast_ref, s_scr, slast_scr, mlast_scr, *, tq):
    seq = q_ref.shape[2]
    nq = seq // tq
    lane = lax.broadcasted_iota(jnp.int32, (tq, LANES), 1)
    ones_col = jnp.where(lane == 0, 1.0, 0.0).astype(BF16)

    half = tq // 2

    def probs(s, mb):
        ps = [jnp.exp2(s[:, c * LANES:(c + 1) * LANES] - mb) for c in range(s.shape[1] // LANES)]
        return jnp.concatenate(ps, axis=1).astype(BF16)

    def pv_pass(scores, mb, values, n_kv):
        acc = None
        for j in range(n_kv):
            vv = jnp.concatenate([values[0, 0, j * tq:(j + 1) * tq, :], ones_col], axis=1)
            if j < n_kv - 1:
                d = _dot(probs(scores[j], mb), vv)
            else:
                d = _dot(probs(scores[j, :, :half], mb), vv[:half])
                d_low = _dot(probs(scores[j, half:, half:], mb[half:]), vv[half:])
                d = jnp.concatenate([d[:half], d[half:] + d_low], axis=0)
            acc = d if acc is None else acc + d
        return (acc[:, :V_HEAD_DIM] / acc[:, V_HEAD_DIM:V_HEAD_DIM + 1]).astype(o_ref.dtype)

    def causal(s):
        r = lax.broadcasted_iota(jnp.int32, s.shape, 0)
        c = lax.broadcasted_iota(jnp.int32, s.shape, 1)
        return jnp.where(c <= r, s, NEG_BIG)

    def qk(q, k):
        return lax.dot_general(q, k, (((1,), (1,)), ((), ())), preferred_element_type=F32)

    @pl.when(pl.program_id(0) == 0)
    def _():
        slast_scr[...] = jnp.zeros(slast_scr.shape, F32)
        mlast_scr[...] = jnp.zeros(mlast_scr.shape, F32)

    olast_ref[0, 0] = pv_pass(slast_scr, mlast_scr[...], vprev_ref, nq)

    for t in range(nq):
        rows = slice(t * tq, (t + 1) * tq)
        q = q_ref[0, 0, rows, :]
        scores = s_scr if t < nq - 1 else slast_scr
        m_run = None
        for j in range(t):
            s = qk(q, k_ref[0, 0, j * tq:(j + 1) * tq, :])
            scores[j] = s
            f = _lane_fold(s, jnp.maximum)
            m_run = f if m_run is None else jnp.maximum(m_run, f)
        s_left = causal(qk(q, k_ref[0, 0, t * tq:t * tq + half, :]))
        s_low = causal(qk(q[half:], k_ref[0, 0, t * tq + half:(t + 1) * tq, :]))
        scores[t, :, :half] = s_left
        scores[t, half:, half:] = s_low
        f = _lane_fold(s_left, jnp.maximum)
        f = jnp.concatenate([f[:half], jnp.maximum(f[half:], _lane_fold(s_low, jnp.maximum))], axis=0)
        m_run = f if m_run is None else jnp.maximum(m_run, f)
        mb = jnp.broadcast_to(jnp.max(m_run, axis=-1, keepdims=True), (tq, LANES))
        if t < nq - 1:
            o_ref[0, 0, rows, :] = pv_pass(s_scr, mb, v_ref, t + 1)
        else:
            mlast_scr[...] = mb
            o_ref[0, 0, rows, :] = jnp.zeros((tq, V_HEAD_DIM), o_ref.dtype)


def _attention(q, k, v, tq):
    b, nh, s, _ = q.shape
    n_heads_total = b * nh

    def cur(n):
        m = jnp.minimum(n, n_heads_total - 1)
        return (m // nh, m % nh, 0, 0)

    def prev(n):
        m = jnp.maximum(n - 1, 0)
        return (m // nh, m % nh, 0, 0)

    return pl.pallas_call(
        functools.partial(_attn_kernel, tq=tq),
        grid=(n_heads_total + 1,),
        in_specs=[
            pl.BlockSpec((1, 1, s, HEAD_SLOT), cur),
            pl.BlockSpec((1, 1, s, HEAD_SLOT), cur),
            pl.BlockSpec((1, 1, s, V_HEAD_DIM), cur),
            pl.BlockSpec((1, 1, s, V_HEAD_DIM), prev),
        ],
        out_specs=[
            pl.BlockSpec((1, 1, s, V_HEAD_DIM), cur),
            pl.BlockSpec((1, 1, tq, V_HEAD_DIM), prev),
        ],
        out_shape=[
            jax.ShapeDtypeStruct((b, nh, s, V_HEAD_DIM), BF16),
            jax.ShapeDtypeStruct((b, nh, tq, V_HEAD_DIM), BF16),
        ],
        scratch_shapes=[
            pltpu.VMEM((s // tq - 1, tq, tq), F32),
            pltpu.VMEM((s // tq, tq, tq), F32),
            pltpu.VMEM((tq, LANES), F32),
        ],
        compiler_params=_params("arbitrary"),
        name="attention",
    )(q, k, v, v)


def _oproj_kernel(o_ref, olast_ref, x_ref, gate_ref, wo_ref, out_ref, cat_scr):
    last = pl.program_id(1) == pl.num_programs(1) - 1
    for hd in range(N_HEADS):
        cat_scr[:, hd * V_HEAD_DIM:(hd + 1) * V_HEAD_DIM] = jnp.where(last, olast_ref[0, hd], o_ref[0, hd])
    out_ref[0] = x_ref[0] + gate_ref[0] * _dot(cat_scr[...], wo_ref[...])


def _out_proj(o, o_last, x, gate, w_o):
    b, s, d = x.shape
    tm = o_last.shape[2]
    return pl.pallas_call(
        _oproj_kernel,
        grid=(b, s // tm),
        in_specs=[
            pl.BlockSpec((1, N_HEADS, tm, V_HEAD_DIM), lambda i, t: (i, 0, t, 0)),
            pl.BlockSpec((1, N_HEADS, tm, V_HEAD_DIM), lambda i, t: (i, 0, 0, 0)),
            pl.BlockSpec((1, tm, d), lambda i, t: (i, t, 0)),
            pl.BlockSpec((1, 1, d), lambda i, t: (i, 0, 0)),
            pl.BlockSpec(w_o.shape, lambda i, t: (0, 0)),
        ],
        out_specs=pl.BlockSpec((1, tm, d), lambda i, t: (i, t, 0)),
        out_shape=jax.ShapeDtypeStruct((b, s, d), F32),
        scratch_shapes=[pltpu.VMEM((tm, N_HEADS * V_HEAD_DIM), BF16)],
        compiler_params=_params("parallel", "parallel"),
        name="out_proj",
    )(o, o_last, x, gate, w_o)


ROW_TILE = 512
FFN_ROWS = 1024
Q_ROWS = 1024
POOL_ROWS = 1024
KV_ROWS = 1024
FFN_DOWN_COLS = 512
FFN_TILE = 512
MOD_TILE = 256
CAST_ROWS = 256


def _swap_halves(w):
    half = w.shape[-1] // 2
    return jnp.concatenate([w[..., half:], w[..., :half]], axis=-1)


def _query_slots(w_uq):
    r = w_uq.shape[0]
    w = w_uq.reshape(r, N_HEADS, Q_HEAD_DIM)
    rope = w[..., QK_NOPE_DIM:]
    return jnp.concatenate([w[..., :QK_NOPE_DIM], rope, _swap_halves(rope)], axis=-1).reshape(
        r, N_HEADS * HEAD_SLOT)


def kernel(x, c, positions, mod_w, mod_b, norm_mix, norm_ffn, pool_w, pool_scale, kv_mod_w, kv_mod_b,
           kv_in_norm, w_dkv, kv_norm, w_uk, w_uv, w_kr, w_dq, q_norm, w_uq, w_o, ffn_gate, ffn_up,
           ffn_down, final_norm):
    b, s, d = x.shape

    c_rows = jnp.zeros((8, d), F32).at[:b].set(c)
    mod = _modulation(c_rows, mod_w, mod_b[:, None, :], MOD_TILE)[:, :b]
    kv_mod = _modulation(c_rows, kv_mod_w[None], kv_mod_b[None, None, :], MOD_TILE)[0, :b]
    table = _rope_table(positions, ROW_TILE)

    def vecs(a, n):
        return [v[:, None, :] for v in jnp.split(a, n, axis=-1)]

    wg = _cast_retile(ffn_gate, 0, FFN_TILE, CAST_ROWS)
    wu = _cast_retile(ffn_up, 0, FFN_TILE, CAST_ROWS)
    wd = _cast_retile(ffn_down, 0, d, FFN_TILE).reshape(ffn_down.shape[1:])

    kv = None
    for i in range(DEPTH):
        shift_m, scale_m, gate_m, shift_f, scale_f, gate_f = vecs(mod[i], N_MOD)
        if i == N_A:
            kv_shift, kv_scale = vecs(kv_mod, 2)
            w_kr2 = jnp.concatenate([w_kr, _swap_halves(w_kr)], axis=-1).astype(BF16)
            kv = _shared_kv(x, kv_shift, kv_scale, kv_in_norm[None], w_dkv.astype(BF16), kv_norm[None],
                            w_uk.astype(BF16), w_uv.astype(BF16), w_kr2, table, KV_ROWS)
        if i < N_A:
            x = _pool_layer(x, shift_m, scale_m, gate_m, norm_mix[i][None], pool_w[i].astype(BF16),
                            pool_scale[i][None], POOL_ROWS)
        else:
            j = i - N_A
            q = _queries(x, shift_m, scale_m, norm_mix[i][None], w_dq[j].astype(BF16), q_norm[j][None],
                         _query_slots(w_uq[j]).astype(BF16), table, Q_ROWS)
            o, o_last = _attention(q, kv[0], kv[1], ROW_TILE)
            x = _out_proj(o, o_last, x, gate_m, w_o[j].astype(BF16))
        if i < DEPTH - 1:
            x, wg, wu, wd = _ffn_layer(x, shift_f, scale_f, gate_f, norm_ffn[i][None], wg, wu, wd,
                                       final_norm[None], tm=FFN_ROWS, final_norm=False,
                                       next_f32=(ffn_gate, ffn_up, ffn_down, i + 1))
        else:
            x = _ffn_layer(x, shift_f, scale_f, gate_f, norm_ffn[i][None], wg, wu, wd, final_norm[None],
                           tm=FFN_ROWS, final_norm=True)
    return x
```
